```python
import math
import jax, jax.numpy as jnp
from jax import lax
import numpy as np

D_MODEL = 1024
BATCH = 16
SEQ = 4096
DEPTH = 4
DEC_BATCH = 8
DEC_SEQ = 2048
PAST_LEN = 128

N_MIXERS = 2
N_ATTN_LAYERS = (DEPTH + 1) // 2
N_CONV_LAYERS = DEPTH // 2
HEAD_DIM = 64
N_HEADS = D_MODEL // HEAD_DIM
N_KV_HEADS = N_HEADS // 4
Q_PER_KV = N_HEADS // N_KV_HEADS
WINDOW = 128
BLOCK = 128
N_BUCKETS = 32
MAX_DISTANCE = 128
CONV_WIDTH = 3
D_FF = 4 * D_MODEL
PLE_DIM = 256
EPS = 1e-6
NEG_INF = -1e30

kernel_name = "hybrid_swa_shortconv_encoder"


def rms_norm(x, g):
    xf = x.astype(jnp.float32)
    y = xf * lax.rsqrt(jnp.mean(xf * xf, axis=-1, keepdims=True) + EPS)
    return (y * g.astype(jnp.float32)).astype(x.dtype)


def t5_bucket(rel):
    half = N_BUCKETS // 2
    max_exact = half // 2
    ret = jnp.where(rel > 0, half, 0)
    n = jnp.abs(rel)
    nf = jnp.maximum(n, 1).astype(jnp.float32)
    large = max_exact + (jnp.log(nf / max_exact) / math.log(MAX_DISTANCE / max_exact)
                         * (half - max_exact)).astype(jnp.int32)
    large = jnp.minimum(large, half - 1)
    return ret + jnp.where(n < max_exact, n, large)


def band_geometry(rel_bias):
    rel = (jnp.arange(3 * BLOCK)[None, :] - BLOCK) - jnp.arange(BLOCK)[:, None]
    in_window = jnp.abs(rel) <= WINDOW
    bias = rel_bias[t5_bucket(rel)].astype(jnp.float32)
    bias = jnp.transpose(bias, (2, 0, 1)).reshape(N_KV_HEADS, Q_PER_KV, BLOCK, 3 * BLOCK)
    return bias, in_window


def window_attention(h, w_qkv, w_o, sink, band_bias, in_window):
    b, s, _ = h.shape
    nb = s // BLOCK
    qkv = h @ w_qkv
    nq = N_HEADS * HEAD_DIM
    nk = N_KV_HEADS * HEAD_DIM
    q = qkv[..., :nq].reshape(b, nb, BLOCK, N_KV_HEADS, Q_PER_KV, HEAD_DIM)
    k = qkv[..., nq:nq + nk].reshape(b, s, N_KV_HEADS, HEAD_DIM)
    v = qkv[..., nq + nk:].reshape(b, s, N_KV_HEADS, HEAD_DIM)
    pad = ((0, 0), (BLOCK, BLOCK), (0, 0), (0, 0))
    kp = jnp.pad(k, pad)
    vp = jnp.pad(v, pad)
    sink_logit = sink.astype(jnp.float32).reshape(1, N_KV_HEADS, Q_PER_KV, 1, 1)
    scale = HEAD_DIM ** -0.5

    def block_fn(n):
        qb = lax.dynamic_index_in_dim(q, n, axis=1, keepdims=False)
        kb = lax.dynamic_slice_in_dim(kp, n * BLOCK, 3 * BLOCK, axis=1)
        vb = lax.dynamic_slice_in_dim(vp, n * BLOCK, 3 * BLOCK, axis=1)
        key_pos = n * BLOCK - BLOCK + jnp.arange(3 * BLOCK)
        valid = in_window & ((key_pos >= 0) & (key_pos < s))[None, :]
        logits = jnp.einsum('bqgrd,bkgd->bgrqk', qb, kb,
                            preferred_element_type=jnp.float32) * scale + band_bias
        logits = jnp.where(valid, logits, NEG_INF)
        m = jnp.maximum(jnp.max(logits, axis=-1, keepdims=True), sink_logit)
        pexp = jnp.exp(logits - m)
        denom = jnp.sum(pexp, axis=-1, keepdims=True) + jnp.exp(sink_logit - m)
        probs = (pexp / denom).astype(vb.dtype)
        return jnp.einsum('bgrqk,bkgd->bqgrd', probs, vb)

    out = lax.map(block_fn, jnp.arange(nb))
    out = jnp.moveaxis(out, 0, 1).reshape(b, s, nq)
    return out @ w_o


def short_conv(h, w_in, conv_w, w_out):
    s = h.shape[1]
    b_gate, c_gate, u = jnp.split(h @ w_in, 3, axis=-1)
    z = c_gate * u
    half = CONV_WIDTH // 2
    zp = jnp.pad(z, ((0, 0), (half, half), (0, 0)))
    conv = zp[:, 0:s] * conv_w[0]
    for t in range(1, CONV_WIDTH):
        conv = conv + zp[:, t:t + s] * conv_w[t]
    return (b_gate * conv) @ w_out


def trunk(x, p, rel_bias, attn_w_qkv, attn_w_o, attn_sink, conv_w_in, conv_w, conv_w_out,
          mlp_w_up, mlp_w_down, ple_w_gate, ple_w_proj, norm_mix, norm_mlp, norm_ple, final_norm):
    band_bias, in_window = band_geometry(rel_bias)
    for i in range(DEPTH):
        j = i // N_MIXERS
        h = rms_norm(x, norm_mix[i])
        if i % N_MIXERS == 0:
            x = x + window_attention(h, attn_w_qkv[j], attn_w_o[j], attn_sink[j], band_bias, in_window)
        else:
            x = x + short_conv(h, conv_w_in[j], conv_w[j], conv_w_out[j])
        h = rms_norm(x, norm_mlp[i])
        x = x + jnp.square(jax.nn.relu(h @ mlp_w_up[i])) @ mlp_w_down[i]
        gate = jax.nn.sigmoid(rms_norm(x, norm_ple[i]) @ ple_w_gate[i])
        x = x + (p[i] @ ple_w_proj[i]) * gate
    return rms_norm(x, final_norm)


def setup_inputs(seed: int = 0) -> dict:
    key = jax.random.key(seed)
    ks = jax.random.split(key, 20)
    f32 = jnp.float32
    D = D_MODEL
    qkv_w = (N_HEADS + 2 * N_KV_HEADS) * HEAD_DIM

    def nrm(k, shape, scale):
        return jax.random.normal(k, shape, f32) * scale

    return {
        "x_prompt": nrm(ks[0], (BATCH, SEQ, D), 1.0),
        "x_sample": nrm(ks[1], (DEC_BATCH, DEC_SEQ, D), 1.0),
        "p_prompt": nrm(ks[2], (DEPTH, BATCH, SEQ, PLE_DIM), 1.0),
        "p_sample": nrm(ks[3], (DEPTH, DEC_BATCH, DEC_SEQ, PLE_DIM), 1.0),
        "rel_bias": nrm(ks[4], (N_BUCKETS, N_HEADS), 0.3),
        "attn_w_qkv": nrm(ks[5], (N_ATTN_LAYERS, D, qkv_w), D ** -0.5),
        "attn_w_o": nrm(ks[6], (N_ATTN_LAYERS, N_HEADS * HEAD_DIM, D), (N_HEADS * HEAD_DIM) ** -0.5),
        "attn_sink": nrm(ks[7], (N_ATTN_LAYERS, N_HEADS), 0.5),
        "conv_w_in": nrm(ks[8], (N_CONV_LAYERS, D, 3 * D), D ** -0.5),
        "conv_w": nrm(ks[9], (N_CONV_LAYERS, CONV_WIDTH, D), CONV_WIDTH ** -0.5),
        "conv_w_out": nrm(ks[10], (N_CONV_LAYERS, D, D), D ** -0.5),
        "mlp_w_up": nrm(ks[11], (DEPTH, D, D_FF), D ** -0.5),
        "mlp_w_down": nrm(ks[12], (DEPTH, D_FF, D), D_FF ** -0.5),
        "ple_w_gate": nrm(ks[13], (DEPTH, D, D), D ** -0.5),
        "ple_w_proj": nrm(ks[14], (DEPTH, PLE_DIM, D), PLE_DIM ** -0.5),
        "norm_mix": 1.0 + nrm(ks[15], (DEPTH, D), 0.05),
        "norm_mlp": 1.0 + nrm(ks[16], (DEPTH, D), 0.05),
        "norm_ple": 1.0 + nrm(ks[17], (DEPTH, D), 0.05),
        "final_norm": 1.0 + nrm(ks[18], (D,), 0.05),
    }


def reference(x_prompt, x_sample, p_prompt, p_sample, rel_bias, attn_w_qkv, attn_w_o, attn_sink,
              conv_w_in, conv_w, conv_w_out, mlp_w_up, mlp_w_down, ple_w_gate, ple_w_proj,
              norm_mix, norm_mlp, norm_ple, final_norm):
    y_prompt = trunk(x_prompt, p_prompt, rel_bias, attn_w_qkv, attn_w_o, attn_sink, conv_w_in, conv_w,
                     conv_w_out, mlp_w_up, mlp_w_down, ple_w_gate, ple_w_proj,
                     norm_mix, norm_mlp, norm_ple, final_norm)
    y_sample = trunk(x_sample, p_sample, rel_bias, attn_w_qkv, attn_w_o, attn_sink, conv_w_in, conv_w,
                     conv_w_out, mlp_w_up, mlp_w_down, ple_w_gate, ple_w_proj,
                     norm_mix, norm_mlp, norm_ple, final_norm)
    return (y_prompt, y_sample)
```

```python
import functools
import math

import jax
import jax.numpy as jnp
from jax import lax
from jax.experimental import pallas as pl
from jax.experimental.pallas import tpu as pltpu

D_MODEL = 1024
DEPTH = 4
N_MIXERS = 2
HEAD_DIM = 64
N_HEADS = D_MODEL // HEAD_DIM
N_KV_HEADS = N_HEADS // 4
Q_PER_KV = N_HEADS // N_KV_HEADS
WINDOW = 128
BLOCK = 128
N_BUCKETS = 32
MAX_DISTANCE = 128
CONV_WIDTH = 3
D_FF = 4 * D_MODEL
PLE_DIM = 256
EPS = 1e-6
NEG_INF = -1e30

Q_COLS = N_HEADS * HEAD_DIM
KV_COLS = 2 * N_KV_HEADS * HEAD_DIM
QKV_COLS = Q_COLS + KV_COLS

V7X_VMEM_BYTES = 64 * 1024 * 1024
VMEM_LIMIT_BYTES = 56 * 1024 * 1024
BF16_SUBLANES = 16

TOKEN_TILE = 512
FF_CHUNK = 1024
CONV_HALO = BF16_SUBLANES

bf16 = jnp.bfloat16
f32 = jnp.float32


def _rms_norm(x, g):
    return x * lax.rsqrt(jnp.mean(x * x, axis=-1, keepdims=True) + EPS) * g


def _resident(shape):
    return pl.BlockSpec(shape, lambda *_: (0,) * len(shape), pipeline_mode=pl.Buffered(1))


def _params(n_axes=1):
    return pltpu.CompilerParams(
        dimension_semantics=("arbitrary",) * n_axes,
        vmem_limit_bytes=VMEM_LIMIT_BYTES,
    )


def _t5_bucket(rel):
    half = N_BUCKETS // 2
    max_exact = half // 2
    ret = jnp.where(rel > 0, half, 0)
    n = jnp.abs(rel)
    nf = jnp.maximum(n, 1).astype(jnp.float32)
    large = max_exact + (jnp.log(nf / max_exact) / math.log(MAX_DISTANCE / max_exact)
                         * (half - max_exact)).astype(jnp.int32)
    large = jnp.minimum(large, half - 1)
    return ret + jnp.where(n < max_exact, n, large)


def _bias_kernel(bucket_ref, rel_bias_ref, out_ref):
    bucket = bucket_ref[...]
    for g in range(N_KV_HEADS):
        for r in range(Q_PER_KV):
            h = g * Q_PER_KV + r
            acc = jnp.full(bucket.shape, NEG_INF, f32)
            for b in range(N_BUCKETS):
                acc = jnp.where(bucket == b, rel_bias_ref[b, h], acc)
            out_ref[g, r * BLOCK:(r + 1) * BLOCK, :] = acc


def _band_bias(rel_bias):
    rel = (jnp.arange(3 * BLOCK)[None, :] - BLOCK) - jnp.arange(BLOCK)[:, None]
    bucket = jnp.where(jnp.abs(rel) <= WINDOW, _t5_bucket(rel), -1).astype(jnp.int32)
    return pl.pallas_call(
        _bias_kernel,
        out_shape=jax.ShapeDtypeStruct((N_KV_HEADS, Q_PER_KV * BLOCK, 3 * BLOCK), f32),
        in_specs=[pl.BlockSpec(memory_space=pltpu.VMEM), pl.BlockSpec(memory_space=pltpu.SMEM)],
        out_specs=pl.BlockSpec(memory_space=pltpu.VMEM),
        name="band_bias",
    )(bucket, rel_bias.astype(f32))


def _qkv_kernel(x_ref, g_ref, w_ref, o_ref):
    h = _rms_norm(x_ref[...], g_ref[...]).astype(bf16)
    qkv = jnp.dot(h, w_ref[...], preferred_element_type=f32)
    o_ref[:, :Q_COLS] = (qkv[:, :Q_COLS] * (HEAD_DIM ** -0.5)).astype(bf16)
    o_ref[:, Q_COLS:] = qkv[:, Q_COLS:].astype(bf16)


def _qkv_proj(x, g, w):
    t = x.shape[0]
    return pl.pallas_call(
        _qkv_kernel,
        out_shape=jax.ShapeDtypeStruct((t, QKV_COLS), bf16),
        grid=(t // TOKEN_TILE,),
        in_specs=[
            pl.BlockSpec((TOKEN_TILE, D_MODEL), lambda i: (i, 0)),
            _resident((1, D_MODEL)),
            _resident((D_MODEL, QKV_COLS)),
        ],
        out_specs=pl.BlockSpec((TOKEN_TILE, QKV_COLS), lambda i: (i, 0)),
        compiler_params=_params(),
        name="qkv_proj",
    )(x, g, w)


def _attn_kernel(tiles_per_seq, x_ref, qkv_ref, prev_ref, next_ref, bias_ref, sink_ref, wo_ref,
                 o_ref, kv_scr, att_scr):
    tq = x_ref.shape[0]
    n_qb = tq // BLOCK
    i = pl.program_id(0)
    first = (i % tiles_per_seq) == 0
    last = (i % tiles_per_seq) == tiles_per_seq - 1

    kv_scr[0:BLOCK, :] = prev_ref[...]
    kv_scr[BLOCK:BLOCK + tq, :] = qkv_ref[:, Q_COLS:]
    kv_scr[BLOCK + tq:, :] = next_ref[...]

    col = lax.broadcasted_iota(jnp.int32, (Q_PER_KV * BLOCK, 3 * BLOCK), 1)
    row_head = lax.broadcasted_iota(jnp.int32, (Q_PER_KV * BLOCK, 1), 0) // BLOCK

    for j in range(n_qb):
        if j == 0:
            dead = jnp.logical_and(first, col < BLOCK)
        elif j == n_qb - 1:
            dead = jnp.logical_and(last, col >= 2 * BLOCK)
        else:
            dead = None
        if n_qb == 1:
            dead = jnp.logical_or(jnp.logical_and(first, col < BLOCK),
                                  jnp.logical_and(last, col >= 2 * BLOCK))
        for g in range(N_KV_HEADS):
            q = jnp.concatenate(
                [qkv_ref[j * BLOCK:(j + 1) * BLOCK,
                         (g * Q_PER_KV + r) * HEAD_DIM:(g * Q_PER_KV + r + 1) * HEAD_DIM]
                 for r in range(Q_PER_KV)], axis=0)
            k = kv_scr[j * BLOCK:(j + 3) * BLOCK, g * HEAD_DIM:(g + 1) * HEAD_DIM]
            v = kv_scr[j * BLOCK:(j + 3) * BLOCK,
                       (N_KV_HEADS + g) * HEAD_DIM:(N_KV_HEADS + g + 1) * HEAD_DIM]
            logits = lax.dot_general(q, k, (((1,), (1,)), ((), ())),
                                     preferred_element_type=f32) + bias_ref[g]
            if dead is not None:
                logits = jnp.where(dead, NEG_INF, logits)
            sink = jnp.zeros((Q_PER_KV * BLOCK, 1), f32)
            for r in range(Q_PER_KV):
                sink = jnp.where(row_head == r, sink_ref[g * Q_PER_KV + r], sink)
            m = jnp.maximum(jnp.max(logits, axis=-1, keepdims=True), sink)
            p = jnp.exp(logits - m)
            denom = jnp.sum(p, axis=-1, keepdims=True) + jnp.exp(sink - m)
            out = jnp.dot(p.astype(bf16), v, preferred_element_type=f32) / denom
            for r in range(Q_PER_KV):
                hcol = (g * Q_PER_KV + r) * HEAD_DIM
                att_scr[j * BLOCK:(j + 1) * BLOCK, hcol:hcol + HEAD_DIM] = (
                    out[r * BLOCK:(r + 1) * BLOCK].astype(bf16))

    o_ref[...] = x_ref[...] + jnp.dot(att_scr[...], wo_ref[...], preferred_element_type=f32)


def _attention(x, qkv, bias, sink, wo, seq_len):
    t = x.shape[0]
    tq = TOKEN_TILE
    tiles_per_seq = seq_len // tq
    blocks_per_tile = tq // BLOCK
    n_blocks = t // BLOCK
    kv_col_block = Q_COLS // KV_COLS
    return pl.pallas_call(
        functools.partial(_attn_kernel, tiles_per_seq),
        out_shape=jax.ShapeDtypeStruct((t, D_MODEL), f32),
        grid=(t // tq,),
        in_specs=[
            pl.BlockSpec((tq, D_MODEL), lambda i: (i, 0)),
            pl.BlockSpec((tq, QKV_COLS), lambda i: (i, 0)),
            pl.BlockSpec((BLOCK, KV_COLS),
                         lambda i: (jnp.maximum(i * blocks_per_tile - 1, 0), kv_col_block)),
            pl.BlockSpec((BLOCK, KV_COLS),
                         lambda i: (jnp.minimum((i + 1) * blocks_per_tile, n_blocks - 1), kv_col_block)),
            _resident((N_KV_HEADS, Q_PER_KV * BLOCK, 3 * BLOCK)),
            pl.BlockSpec(memory_space=pltpu.SMEM),
            _resident((Q_COLS, D_MODEL)),
        ],
        out_specs=pl.BlockSpec((tq, D_MODEL), lambda i: (i, 0)),
        scratch_shapes=[
            pltpu.VMEM((tq + 2 * BLOCK, KV_COLS), bf16),
            pltpu.VMEM((tq, Q_COLS), bf16),
        ],
        compiler_params=_params(),
        name="band_attention",
    )(x, qkv, qkv, qkv, bias, sink, wo)


def _conv_kernel(tiles_per_seq, x_ref, prev_ref, next_ref, g_ref, wb_ref, wcu_ref, cw_ref, wout_ref,
                 o_ref):
    tm = x_ref.shape[0]
    i = pl.program_id(0)
    first = (i % tiles_per_seq) == 0
    last = (i % tiles_per_seq) == tiles_per_seq - 1
    g = g_ref[...]
    x = x_ref[...]
    h = _rms_norm(x, g).astype(bf16)
    h_ext = jnp.concatenate(
        [_rms_norm(prev_ref[...], g).astype(bf16), h, _rms_norm(next_ref[...], g).astype(bf16)], axis=0)
    cu = jnp.dot(h_ext, wcu_ref[...], preferred_element_type=f32)
    z = cu[:, :D_MODEL] * cu[:, D_MODEL:]
    row = lax.broadcasted_iota(jnp.int32, (tm + 2 * CONV_HALO, 1), 0)
    pad = jnp.logical_or(jnp.logical_and(first, row < CONV_HALO),
                         jnp.logical_and(last, row >= tm + CONV_HALO))
    z = jnp.where(pad, 0.0, z)
    n_ext = tm + 2 * CONV_HALO
    z_prev = pltpu.roll(z, 1, axis=0)[CONV_HALO:CONV_HALO + tm]
    z_next = pltpu.roll(z, n_ext - 1, axis=0)[CONV_HALO:CONV_HALO + tm]
    cw = cw_ref[...]
    conv = z_prev * cw[0:1] + z[CONV_HALO:CONV_HALO + tm] * cw[1:2] + z_next * cw[2:3]
    b_gate = jnp.dot(h, wb_ref[...], preferred_element_type=f32)
    o_ref[...] = x + jnp.dot((b_gate * conv).astype(bf16), wout_ref[...], preferred_element_type=f32)


def _short_conv(x, g, w_b, w_cu, conv_w, w_out, seq_len):
    t = x.shape[0]
    tm = TOKEN_TILE
    tiles_per_seq = seq_len // tm
    halo_per_tile = tm // CONV_HALO
    n_halo_blocks = t // CONV_HALO
    return pl.pallas_call(
        functools.partial(_conv_kernel, tiles_per_seq),
        out_shape=jax.ShapeDtypeStruct((t, D_MODEL), f32),
        grid=(t // tm,),
        in_specs=[
            pl.BlockSpec((tm, D_MODEL), lambda i: (i, 0)),
            pl.BlockSpec((CONV_HALO, D_MODEL), lambda i: (jnp.maximum(i * halo_per_tile - 1, 0), 0)),
            pl.BlockSpec((CONV_HALO, D_MODEL),
                         lambda i: (jnp.minimum((i + 1) * halo_per_tile, n_halo_blocks - 1), 0)),
            _resident((1, D_MODEL)),
            _resident((D_MODEL, D_MODEL)),
            _resident((D_MODEL, 2 * D_MODEL)),
            _resident((CONV_WIDTH, D_MODEL)),
            _resident((D_MODEL, D_MODEL)),
        ],
        out_specs=pl.BlockSpec((tm, D_MODEL), lambda i: (i, 0)),
        compiler_params=_params(),
        name="short_conv",
    )(x, x, x, g, w_b, w_cu, conv_w, w_out)


def _mlp_kernel(apply_final_norm, x_ref, p_ref, gm_ref, gp_ref, gf_ref, up_ref, down_ref, wg_ref, wp_ref,
                o_ref):
    x = x_ref[...]
    h = _rms_norm(x, gm_ref[...]).astype(bf16)
    acc = x
    for c in range(D_FF // FF_CHUNK):
        a = jnp.dot(h, up_ref[:, c * FF_CHUNK:(c + 1) * FF_CHUNK], preferred_element_type=f32)
        a = jnp.square(jnp.maximum(a, 0.0)).astype(bf16)
        acc = acc + jnp.dot(a, down_ref[c * FF_CHUNK:(c + 1) * FF_CHUNK, :], preferred_element_type=f32)
    x1 = acc
    z = jnp.dot(_rms_norm(x1, gp_ref[...]).astype(bf16), wg_ref[...], preferred_element_type=f32)
    gate = 1.0 / (1.0 + jnp.exp(-z))
    pp = jnp.dot(p_ref[...].astype(bf16), wp_ref[...], preferred_element_type=f32)
    y = x1 + pp * gate
    if apply_final_norm:
        y = _rms_norm(y, gf_ref[...])
    o_ref[...] = y


def _mlp_ple(x, p, g_mlp, g_ple, g_final, w_up, w_down, w_gate, w_proj, apply_final_norm):
    t = x.shape[0]
    tm = TOKEN_TILE
    return pl.pallas_call(
        functools.partial(_mlp_kernel, apply_final_norm),
        out_shape=jax.ShapeDtypeStruct((t, D_MODEL), f32),
        grid=(t // tm,),
        in_specs=[
            pl.BlockSpec((tm, D_MODEL), lambda i: (i, 0)),
            pl.BlockSpec((tm, PLE_DIM), lambda i: (i, 0)),
            _resident((1, D_MODEL)),
            _resident((1, D_MODEL)),
            _resident((1, D_MODEL)),
            _resident((D_MODEL, D_FF)),
            _resident((D_FF, D_MODEL)),
            _resident((D_MODEL, D_MODEL)),
            _resident((PLE_DIM, D_MODEL)),
        ],
        out_specs=pl.BlockSpec((tm, D_MODEL), lambda i: (i, 0)),
        compiler_params=_params(),
        name="mlp_ple",
    )(x, p, g_mlp, g_ple, g_final, w_up, w_down, w_gate, w_proj)


def _trunk(x, p, band_bias, w):
    b, s, _ = x.shape
    t = b * s
    x = x.reshape(t, D_MODEL)
    p = p.reshape(DEPTH, t, PLE_DIM)
    for i in range(DEPTH):
        j = i // N_MIXERS
        g_mix = w["norm_mix"][i][None, :]
        if i % N_MIXERS == 0:
            qkv = _qkv_proj(x, g_mix, w["attn_w_qkv"][j])
            x = _attention(x, qkv, band_bias, w["attn_sink"][j], w["attn_w_o"][j], s)
        else:
            x = _short_conv(x, g_mix, w["conv_w_b"][j], w["conv_w_cu"][j], w["conv_w"][j],
                            w["conv_w_out"][j], s)
        x = _mlp_ple(x, p[i], w["norm_mlp"][i][None, :], w["norm_ple"][i][None, :],
                     w["final_norm"][None, :], w["mlp_w_up"][i], w["mlp_w_down"][i],
                     w["ple_w_gate"][i], w["ple_w_proj"][i], apply_final_norm=(i == DEPTH - 1))
    return x.reshape(b, s, D_MODEL)


def kernel(x_prompt, x_sample, p_prompt, p_sample, rel_bias, attn_w_qkv, attn_w_o, attn_sink,
           conv_w_in, conv_w, conv_w_out, mlp_w_up, mlp_w_down, ple_w_gate, ple_w_proj,
           norm_mix, norm_mlp, norm_ple, final_norm):
    w = {
        "attn_w_qkv": attn_w_qkv.astype(bf16),
        "attn_w_o": attn_w_o.astype(bf16),
        "attn_sink": attn_sink.astype(f32),
        "conv_w_b": conv_w_in[:, :, :D_MODEL].astype(bf16),
        "conv_w_cu": conv_w_in[:, :, D_MODEL:].astype(bf16),
        "conv_w": conv_w.astype(f32),
        "conv_w_out": conv_w_out.astype(bf16),
        "mlp_w_up": mlp_w_up.astype(bf16),
        "mlp_w_down": mlp_w_down.astype(bf16),
        "ple_w_gate": ple_w_gate.astype(bf16),
        "ple_w_proj": ple_w_proj.astype(bf16),
        "norm_mix": norm_mix.astype(f32),
        "norm_mlp": norm_mlp.astype(f32),
        "norm_ple": norm_ple.astype(f32),
        "final_norm": final_norm.astype(f32),
    }
    band_bias = _band_bias(rel_bias)
    y_prompt = _trunk(x_prompt, p_prompt, band_bias, w)
    y_sample = _trunk(x_sample, p_sample, band_bias, w)
    return (y_prompt, y_sample)
```

```python
import functools
import math

import jax
import jax.numpy as jnp
from jax import lax
from jax.experimental import pallas as pl
from jax.experimental.pallas import tpu as pltpu

D_MODEL = 1024
DEPTH = 4
N_MIXERS = 2
HEAD_DIM = 64
N_HEADS = D_MODEL // HEAD_DIM
N_KV_HEADS = N_HEADS // 4
Q_PER_KV = N_HEADS // N_KV_HEADS
WINDOW = 128
BLOCK = 128
N_BUCKETS = 32
MAX_DISTANCE = 128
CONV_WIDTH = 3
D_FF = 4 * D_MODEL
PLE_DIM = 256
EPS = 1e-6
NEG_INF = -1e30
LOG2E = math.log2(math.e)

Q_COLS = N_HEADS * HEAD_DIM
K_COLS = N_KV_HEADS * HEAD_DIM
KV_COLS = 2 * K_COLS
QKV_COLS = Q_COLS + KV_COLS
GROUP_ROWS = Q_PER_KV * BLOCK
BAND = 3 * BLOCK

VMEM_LIMIT_BYTES = 56 * 1024 * 1024
BF16_SUBLANES = 16
LANES = 128

TOKEN_TILE = 512
FF_CHUNK = 1024
CONV_HALO = BF16_SUBLANES

bf16 = jnp.bfloat16
f32 = jnp.float32


def _rms_norm(x, g):
    return x * lax.rsqrt(jnp.mean(x * x, axis=-1, keepdims=True) + EPS) * g


def _resident(shape):
    return pl.BlockSpec(shape, lambda *_: (0,) * len(shape), pipeline_mode=pl.Buffered(1))


def _params(n_axes=1):
    return pltpu.CompilerParams(
        dimension_semantics=("arbitrary",) * n_axes,
        vmem_limit_bytes=VMEM_LIMIT_BYTES,
    )


def _t5_bucket(rel):
    half = N_BUCKETS // 2
    max_exact = half // 2
    ret = jnp.where(rel > 0, half, 0)
    n = jnp.abs(rel)
    nf = jnp.maximum(n, 1).astype(jnp.float32)
    large = max_exact + (jnp.log(nf / max_exact) / math.log(MAX_DISTANCE / max_exact)
                         * (half - max_exact)).astype(jnp.int32)
    large = jnp.minimum(large, half - 1)
    return ret + jnp.where(n < max_exact, n, large)


def _bias_kernel(bucket_ref, rel_bias_ref, out_ref):
    bucket = bucket_ref[...]
    for g in range(N_KV_HEADS):
        for r in range(Q_PER_KV):
            h = g * Q_PER_KV + r
            acc = jnp.full(bucket.shape, NEG_INF, f32)
            for b in range(N_BUCKETS):
                acc = jnp.where(bucket == b, rel_bias_ref[b, h] * LOG2E, acc)
            out_ref[g, :, r * BLOCK:(r + 1) * BLOCK] = acc


def _band_bias(rel_bias):
    rel = (jnp.arange(BAND)[:, None] - BLOCK) - jnp.arange(BLOCK)[None, :]
    bucket = jnp.where(jnp.abs(rel) <= WINDOW, _t5_bucket(rel), -1).astype(jnp.int32)
    return pl.pallas_call(
        _bias_kernel,
        out_shape=jax.ShapeDtypeStruct((N_KV_HEADS, BAND, GROUP_ROWS), f32),
        in_specs=[pl.BlockSpec(memory_space=pltpu.VMEM), pl.BlockSpec(memory_space=pltpu.SMEM)],
        out_specs=pl.BlockSpec(memory_space=pltpu.VMEM),
        name="band_bias",
    )(bucket, rel_bias.astype(f32))


def _qkv_kernel(x_ref, g_ref, w_ref, o_ref):
    h = _rms_norm(x_ref[...], g_ref[...]).astype(bf16)
    qkv = jnp.dot(h, w_ref[...], preferred_element_type=f32)
    o_ref[:, :Q_COLS] = (qkv[:, :Q_COLS] * (HEAD_DIM ** -0.5 * LOG2E)).astype(bf16)
    o_ref[:, Q_COLS:] = qkv[:, Q_COLS:].astype(bf16)


def _qkv_proj(x, g, w):
    t = x.shape[0]
    return pl.pallas_call(
        _qkv_kernel,
        out_shape=jax.ShapeDtypeStruct((t, QKV_COLS), bf16),
        grid=(t // TOKEN_TILE,),
        in_specs=[
            pl.BlockSpec((TOKEN_TILE, D_MODEL), lambda i: (i, 0)),
            _resident((1, D_MODEL)),
            _resident((D_MODEL, QKV_COLS)),
        ],
        out_specs=pl.BlockSpec((TOKEN_TILE, QKV_COLS), lambda i: (i, 0)),
        compiler_params=_params(),
        name="qkv_proj",
    )(x, g, w)


def _attn_kernel(tiles_per_seq, qkv_ref, prev_ref, next_ref, bias_ref, sink_ref, o_ref, k_scr, vt_scr):
    tq = qkv_ref.shape[0]
    n_qb = tq // BLOCK
    i = pl.program_id(0)
    first = (i % tiles_per_seq) == 0
    last = (i % tiles_per_seq) == tiles_per_seq - 1

    k_scr[0:BLOCK, :] = prev_ref[:, :K_COLS]
    k_scr[BLOCK:BLOCK + tq, :] = qkv_ref[:, Q_COLS:Q_COLS + K_COLS]
    k_scr[BLOCK + tq:, :] = next_ref[:, :K_COLS]
    v_all = jnp.concatenate(
        [prev_ref[:, K_COLS:], qkv_ref[:, Q_COLS + K_COLS:], next_ref[:, K_COLS:]], axis=0)
    v_t = v_all.T
    ones = jnp.ones((LANES - HEAD_DIM, tq + 2 * BLOCK), bf16)
    for g in range(N_KV_HEADS):
        vt_scr[g, 0:HEAD_DIM, :] = v_t[g * HEAD_DIM:(g + 1) * HEAD_DIM, :]
        vt_scr[g, HEAD_DIM:, :] = ones

    lane_head = lax.broadcasted_iota(jnp.int32, (1, GROUP_ROWS), 1) // BLOCK

    def scores(j, g):
        q = jnp.concatenate(
            [qkv_ref[j * BLOCK:(j + 1) * BLOCK,
                     (g * Q_PER_KV + r) * HEAD_DIM:(g * Q_PER_KV + r + 1) * HEAD_DIM]
             for r in range(Q_PER_KV)], axis=0)
        k = k_scr[j * BLOCK:j * BLOCK + BAND, g * HEAD_DIM:(g + 1) * HEAD_DIM]
        s = lax.dot_general(k, q, (((1,), (1,)), ((), ())),
                            preferred_element_type=f32) + bias_ref[g]
        if j == 0:
            s = jnp.concatenate([jnp.where(first, NEG_INF, s[:BLOCK]), s[BLOCK:]], axis=0)
        if j == n_qb - 1:
            s = jnp.concatenate([s[:2 * BLOCK], jnp.where(last, NEG_INF, s[2 * BLOCK:])], axis=0)
        return s

    def finish(j, g, s):
        sink = jnp.zeros((1, GROUP_ROWS), f32)
        for r in range(Q_PER_KV):
            sink = jnp.where(lane_head == r, sink_ref[g * Q_PER_KV + r] * LOG2E, sink)
        m = jnp.maximum(jnp.max(s, axis=0, keepdims=True), sink)
        p = jnp.exp2(s - m).astype(bf16)
        ot = jnp.dot(vt_scr[g, :, j * BLOCK:j * BLOCK + BAND], p, preferred_element_type=f32)
        denom = ot[HEAD_DIM:HEAD_DIM + 1, :] + jnp.exp2(sink - m)
        out_t = ot[:HEAD_DIM, :] * (1.0 / denom)
        for c in range(Q_PER_KV // 2):
            pair = jnp.concatenate([out_t[:, (2 * c) * BLOCK:(2 * c + 1) * BLOCK],
                                    out_t[:, (2 * c + 1) * BLOCK:(2 * c + 2) * BLOCK]], axis=0)
            col = (g * Q_PER_KV + 2 * c) * HEAD_DIM
            o_ref[j * BLOCK:(j + 1) * BLOCK, col:col + 2 * HEAD_DIM] = pair.T.astype(bf16)

    steps = [(j, g) for j in range(n_qb) for g in range(N_KV_HEADS)]
    s_next = scores(*steps[0])
    for n, (j, g) in enumerate(steps):
        s = s_next
        if n + 1 < len(steps):
            s_next = scores(*steps[n + 1])
        finish(j, g, s)


def _attention(qkv, bias, sink, seq_len):
    t = qkv.shape[0]
    tq = TOKEN_TILE
    tiles_per_seq = seq_len // tq
    blocks_per_tile = tq // BLOCK
    n_blocks = t // BLOCK
    kv_col_block = Q_COLS // KV_COLS
    return pl.pallas_call(
        functools.partial(_attn_kernel, tiles_per_seq),
        out_shape=jax.ShapeDtypeStruct((t, Q_COLS), bf16),
        grid=(t // tq,),
        in_specs=[
            pl.BlockSpec((tq, QKV_COLS), lambda i: (i, 0)),
            pl.BlockSpec((BLOCK, KV_COLS),
                         lambda i: (jnp.maximum(i * blocks_per_tile - 1, 0), kv_col_block)),
            pl.BlockSpec((BLOCK, KV_COLS),
                         lambda i: (jnp.minimum((i + 1) * blocks_per_tile, n_blocks - 1), kv_col_block)),
            _resident((N_KV_HEADS, BAND, GROUP_ROWS)),
            pl.BlockSpec(memory_space=pltpu.SMEM),
        ],
        out_specs=pl.BlockSpec((tq, Q_COLS), lambda i: (i, 0)),
        scratch_shapes=[
            pltpu.VMEM((tq + 2 * BLOCK, K_COLS), bf16),
            pltpu.VMEM((N_KV_HEADS, LANES, tq + 2 * BLOCK), bf16),
        ],
        compiler_params=_params(),
        name="band_attention",
    )(qkv, qkv, qkv, bias, sink)


def _conv_kernel(tiles_per_seq, x_ref, prev_ref, next_ref, g_ref, wb_ref, wcu_ref, cw_ref, wout_ref,
                 o_ref):
    tm = x_ref.shape[0]
    i = pl.program_id(0)
    first = (i % tiles_per_seq) == 0
    last = (i % tiles_per_seq) == tiles_per_seq - 1
    g = g_ref[...]
    x = x_ref[...]
    h = _rms_norm(x, g).astype(bf16)
    h_ext = jnp.concatenate(
        [_rms_norm(prev_ref[...], g).astype(bf16), h, _rms_norm(next_ref[...], g).astype(bf16)], axis=0)
    cu = jnp.dot(h_ext, wcu_ref[...], preferred_element_type=f32)
    z = cu[:, :D_MODEL] * cu[:, D_MODEL:]
    row = lax.broadcasted_iota(jnp.int32, (tm + 2 * CONV_HALO, 1), 0)
    pad = jnp.logical_or(jnp.logical_and(first, row < CONV_HALO),
                         jnp.logical_and(last, row >= tm + CONV_HALO))
    z = jnp.where(pad, 0.0, z)
    n_ext = tm + 2 * CONV_HALO
    z_prev = pltpu.roll(z, 1, axis=0)[CONV_HALO:CONV_HALO + tm]
    z_next = pltpu.roll(z, n_ext - 1, axis=0)[CONV_HALO:CONV_HALO + tm]
    cw = cw_ref[...]
    conv = z_prev * cw[0:1] + z[CONV_HALO:CONV_HALO + tm] * cw[1:2] + z_next * cw[2:3]
    b_gate = jnp.dot(h, wb_ref[...], preferred_element_type=f32)
    o_ref[...] = x + jnp.dot((b_gate * conv).astype(bf16), wout_ref[...], preferred_element_type=f32)


def _short_conv(x, g, w_b, w_cu, conv_w, w_out, seq_len):
    t = x.shape[0]
    tm = TOKEN_TILE
    tiles_per_seq = seq_len // tm
    halo_per_tile = tm // CONV_HALO
    n_halo_blocks = t // CONV_HALO
    return pl.pallas_call(
        functools.partial(_conv_kernel, tiles_per_seq),
        out_shape=jax.ShapeDtypeStruct((t, D_MODEL), f32),
        grid=(t // tm,),
        in_specs=[
            pl.BlockSpec((tm, D_MODEL), lambda i: (i, 0)),
            pl.BlockSpec((CONV_HALO, D_MODEL), lambda i: (jnp.maximum(i * halo_per_tile - 1, 0), 0)),
            pl.BlockSpec((CONV_HALO, D_MODEL),
                         lambda i: (jnp.minimum((i + 1) * halo_per_tile, n_halo_blocks - 1), 0)),
            _resident((1, D_MODEL)),
            _resident((D_MODEL, D_MODEL)),
            _resident((D_MODEL, 2 * D_MODEL)),
            _resident((CONV_WIDTH, D_MODEL)),
            _resident((D_MODEL, D_MODEL)),
        ],
        out_specs=pl.BlockSpec((tm, D_MODEL), lambda i: (i, 0)),
        compiler_params=_params(),
        name="short_conv",
    )(x, x, x, g, w_b, w_cu, conv_w, w_out)


def _mlp_body(x, p_ref, gm_ref, gp_ref, gf_ref, up_ref, down_ref, wg_ref, wp_ref, o_ref, apply_final_norm):
    h = _rms_norm(x, gm_ref[...]).astype(bf16)
    acc = x
    for c in range(D_FF // FF_CHUNK):
        a = jnp.dot(h, up_ref[:, c * FF_CHUNK:(c + 1) * FF_CHUNK], preferred_element_type=f32)
        a = jnp.square(jnp.maximum(a, 0.0)).astype(bf16)
        acc = acc + jnp.dot(a, down_ref[c * FF_CHUNK:(c + 1) * FF_CHUNK, :], preferred_element_type=f32)
    x1 = acc
    z = jnp.dot(_rms_norm(x1, gp_ref[...]).astype(bf16), wg_ref[...], preferred_element_type=f32)
    gate = 1.0 / (1.0 + jnp.exp(-z))
    pp = jnp.dot(p_ref[...].astype(bf16), wp_ref[...], preferred_element_type=f32)
    y = x1 + pp * gate
    if apply_final_norm:
        y = _rms_norm(y, gf_ref[...])
    o_ref[...] = y


def _mlp_kernel(apply_final_norm, x_ref, *refs):
    _mlp_body(x_ref[...], *refs, apply_final_norm=apply_final_norm)


def _attn_mlp_kernel(apply_final_norm, x_ref, att_ref, wo_ref, *refs):
    x = x_ref[...] + jnp.dot(att_ref[...], wo_ref[...], preferred_element_type=f32)
    _mlp_body(x, *refs, apply_final_norm=apply_final_norm)


def _mlp_ple(x, att, w_o, p, layer, g_mlp, g_ple, g_final, w_up, w_down, w_gate, w_proj, apply_final_norm):
    t = x.shape[0]
    tm = TOKEN_TILE
    row_tile = pl.BlockSpec((tm, D_MODEL), lambda i: (i, 0))
    if att is None:
        body, lead_args, lead_specs = _mlp_kernel, (x,), [row_tile]
    else:
        body = _attn_mlp_kernel
        lead_args = (x, att, w_o)
        lead_specs = [row_tile, pl.BlockSpec((tm, Q_COLS), lambda i: (i, 0)), _resident((Q_COLS, D_MODEL))]
    return pl.pallas_call(
        functools.partial(body, apply_final_norm),
        out_shape=jax.ShapeDtypeStruct((t, D_MODEL), f32),
        grid=(t // tm,),
        in_specs=lead_specs + [
            pl.BlockSpec((None, tm, PLE_DIM), lambda i: (layer, i, 0)),
            _resident((1, D_MODEL)),
            _resident((1, D_MODEL)),
            _resident((1, D_MODEL)),
            _resident((D_MODEL, D_FF)),
            _resident((D_FF, D_MODEL)),
            _resident((D_MODEL, D_MODEL)),
            _resident((PLE_DIM, D_MODEL)),
        ],
        out_specs=row_tile,
        compiler_params=_params(),
        name="mlp_ple",
    )(*lead_args, p, g_mlp, g_ple, g_final, w_up, w_down, w_gate, w_proj)


def _trunk(x, p, band_bias, w):
    b, s, _ = x.shape
    t = b * s
    x = x.reshape(t, D_MODEL)
    p = p.reshape(DEPTH, t, PLE_DIM)
    for i in range(DEPTH):
        j = i // N_MIXERS
        g_mix = w["norm_mix"][i][None, :]
        att = w_o = None
        if i % N_MIXERS == 0:
            qkv = _qkv_proj(x, g_mix, w["attn_w_qkv"][j])
            att = _attention(qkv, band_bias, w["attn_sink"][j], s)
            w_o = w["attn_w_o"][j]
        else:
            x = _short_conv(x, g_mix, w["conv_w_b"][j], w["conv_w_cu"][j], w["conv_w"][j],
                            w["conv_w_out"][j], s)
        x = _mlp_ple(x, att, w_o, p, i, w["norm_mlp"][i][None, :], w["norm_ple"][i][None, :],
                     w["final_norm"][None, :], w["mlp_w_up"][i], w["mlp_w_down"][i],
                     w["ple_w_gate"][i], w["ple_w_proj"][i], apply_final_norm=(i == DEPTH - 1))
    return x.reshape(b, s, D_MODEL)


def kernel(x_prompt, x_sample, p_prompt, p_sample, rel_bias, attn_w_qkv, attn_w_o, attn_sink,
           conv_w_in, conv_w, conv_w_out, mlp_w_up, mlp_w_down, ple_w_gate, ple_w_proj,
           norm_mix, norm_mlp, norm_ple, final_norm):
    w = {
        "attn_w_qkv": attn_w_qkv.astype(bf16),
        "attn_w_o": attn_w_o.astype(bf16),
        "attn_sink": attn_sink.astype(f32),
        "conv_w_b": conv_w_in[:, :, :D_MODEL].astype(bf16),
        "conv_w_cu": conv_w_in[:, :, D_MODEL:].astype(bf16),
        "conv_w": conv_w.astype(f32),
        "conv_w_out": conv_w_out.astype(bf16),
        "mlp_w_up": mlp_w_up.astype(bf16),
        "mlp_w_down": mlp_w_down.astype(bf16),
        "ple_w_gate": ple_w_gate.astype(bf16),
        "ple_w_proj": ple_w_proj.astype(bf16),
        "norm_mix": norm_mix.astype(f32),
        "norm_mlp": norm_mlp.astype(f32),
        "norm_ple": norm_ple.astype(f32),
        "final_norm": final_norm.astype(f32),
    }
    band_bias = _band_bias(rel_bias)
    y_prompt = _trunk(x_prompt, p_prompt, band_bias, w)
    y_sample = _trunk(x_sample, p_sample, band_bias, w)
    return (y_prompt, y_sample)
```

```python
import functools
import math

import jax
import jax.numpy as jnp
from jax import lax
from jax.experimental import pallas as pl
from jax.experimental.pallas import tpu as pltpu

D_MODEL = 1024
DEPTH = 4
N_MIXERS = 2
HEAD_DIM = 64
N_HEADS = D_MODEL // HEAD_DIM
N_KV_HEADS = N_HEADS // 4
Q_PER_KV = N_HEADS // N_KV_HEADS
WINDOW = 128
BLOCK = 128
N_BUCKETS = 32
MAX_DISTANCE = 128
CONV_WIDTH = 3
D_FF = 4 * D_MODEL
PLE_DIM = 256
EPS = 1e-6
NEG_INF = -1e30
LOG2E = math.log2(math.e)

Q_COLS = N_HEADS * HEAD_DIM
K_COLS = N_KV_HEADS * HEAD_DIM
KV_COLS = 2 * K_COLS
QKV_COLS = Q_COLS + KV_COLS
GROUP_ROWS = Q_PER_KV * BLOCK
BAND = 3 * BLOCK

VMEM_LIMIT_BYTES = 56 * 1024 * 1024
BF16_SUBLANES = 16
F32_SUBLANES = 8
LANES = 128

TOKEN_TILE = 1024
SUB_ROWS = 256
FF_CHUNK = 1024
CONV_HALO = BF16_SUBLANES

bf16 = jnp.bfloat16
f32 = jnp.float32


def _rms_norm(x, g):
    return x * lax.rsqrt(jnp.mean(x * x, axis=-1, keepdims=True) + EPS) * g


def _resident(shape):
    return pl.BlockSpec(shape, lambda *_: (0,) * len(shape), pipeline_mode=pl.Buffered(1))


def _params(n_axes=1):
    return pltpu.CompilerParams(
        dimension_semantics=("arbitrary",) * n_axes,
        vmem_limit_bytes=VMEM_LIMIT_BYTES,
    )


def _t5_bucket(rel):
    half = N_BUCKETS // 2
    max_exact = half // 2
    ret = jnp.where(rel > 0, half, 0)
    n = jnp.abs(rel)
    nf = jnp.maximum(n, 1).astype(jnp.float32)
    large = max_exact + (jnp.log(nf / max_exact) / math.log(MAX_DISTANCE / max_exact)
                         * (half - max_exact)).astype(jnp.int32)
    large = jnp.minimum(large, half - 1)
    return ret + jnp.where(n < max_exact, n, large)


def _bias_kernel(bucket_ref, rel_bias_ref, out_ref):
    bucket = bucket_ref[...]
    for g in range(N_KV_HEADS):
        for r in range(Q_PER_KV):
            h = g * Q_PER_KV + r
            acc = jnp.full(bucket.shape, NEG_INF, f32)
            for b in range(N_BUCKETS):
                acc = jnp.where(bucket == b, rel_bias_ref[b, h] * LOG2E, acc)
            out_ref[g, :, r * BLOCK:(r + 1) * BLOCK] = acc


def _band_bias(rel_bias):
    rel = (jnp.arange(BAND)[:, None] - BLOCK) - jnp.arange(BLOCK)[None, :]
    bucket = jnp.where(jnp.abs(rel) <= WINDOW, _t5_bucket(rel), -1).astype(jnp.int32)
    return pl.pallas_call(
        _bias_kernel,
        out_shape=jax.ShapeDtypeStruct((N_KV_HEADS, BAND, GROUP_ROWS), f32),
        in_specs=[pl.BlockSpec(memory_space=pltpu.VMEM), pl.BlockSpec(memory_space=pltpu.SMEM)],
        out_specs=pl.BlockSpec(memory_space=pltpu.VMEM),
        name="band_bias",
    )(bucket, rel_bias.astype(f32))


def _qkv_kernel(x_ref, g_ref, w_ref, o_ref):
    for s in range(x_ref.shape[0] // SUB_ROWS):
        rows = slice(s * SUB_ROWS, (s + 1) * SUB_ROWS)
        h = _rms_norm(x_ref[rows, :], g_ref[...]).astype(bf16)
        qkv = jnp.dot(h, w_ref[...], preferred_element_type=f32)
        o_ref[rows, :Q_COLS] = (qkv[:, :Q_COLS] * (HEAD_DIM ** -0.5 * LOG2E)).astype(bf16)
        o_ref[rows, Q_COLS:] = qkv[:, Q_COLS:].astype(bf16)


def _qkv_proj(x, g, w):
    t = x.shape[0]
    return pl.pallas_call(
        _qkv_kernel,
        out_shape=jax.ShapeDtypeStruct((t, QKV_COLS), bf16),
        grid=(t // TOKEN_TILE,),
        in_specs=[
            pl.BlockSpec((TOKEN_TILE, D_MODEL), lambda i: (i, 0)),
            _resident((1, D_MODEL)),
            _resident((D_MODEL, QKV_COLS)),
        ],
        out_specs=pl.BlockSpec((TOKEN_TILE, QKV_COLS), lambda i: (i, 0)),
        compiler_params=_params(),
        name="qkv_proj",
    )(x, g, w)


def _attn_kernel(tiles_per_seq, qkv_ref, prev_ref, next_ref, bias_ref, sink_ref, o_ref, k_scr, vt_scr):
    tq = qkv_ref.shape[0]
    n_qb = tq // BLOCK
    i = pl.program_id(0)
    first = (i % tiles_per_seq) == 0
    last = (i % tiles_per_seq) == tiles_per_seq - 1

    k_scr[0:BLOCK, :] = prev_ref[:, :K_COLS]
    k_scr[BLOCK:BLOCK + tq, :] = qkv_ref[:, Q_COLS:Q_COLS + K_COLS]
    k_scr[BLOCK + tq:, :] = next_ref[:, :K_COLS]
    v_all = jnp.concatenate(
        [prev_ref[:, K_COLS:], qkv_ref[:, Q_COLS + K_COLS:], next_ref[:, K_COLS:]], axis=0)
    v_t = v_all.T
    ones = jnp.ones((LANES - HEAD_DIM, tq + 2 * BLOCK), bf16)
    for g in range(N_KV_HEADS):
        vt_scr[g, 0:HEAD_DIM, :] = v_t[g * HEAD_DIM:(g + 1) * HEAD_DIM, :]
        vt_scr[g, HEAD_DIM:, :] = ones

    lane_head = lax.broadcasted_iota(jnp.int32, (1, GROUP_ROWS), 1) // BLOCK

    def scores(j, g):
        q = jnp.concatenate(
            [qkv_ref[j * BLOCK:(j + 1) * BLOCK,
                     (g * Q_PER_KV + r) * HEAD_DIM:(g * Q_PER_KV + r + 1) * HEAD_DIM]
             for r in range(Q_PER_KV)], axis=0)
        k = k_scr[j * BLOCK:j * BLOCK + BAND, g * HEAD_DIM:(g + 1) * HEAD_DIM]
        s = lax.dot_general(k, q, (((1,), (1,)), ((), ())),
                            preferred_element_type=f32) + bias_ref[g]
        if j == 0:
            s = jnp.concatenate([jnp.where(first, NEG_INF, s[:BLOCK]), s[BLOCK:]], axis=0)
        if j == n_qb - 1:
            s = jnp.concatenate([s[:2 * BLOCK], jnp.where(last, NEG_INF, s[2 * BLOCK:])], axis=0)
        return s

    def finish(j, g, s):
        sink = jnp.zeros((1, GROUP_ROWS), f32)
        for r in range(Q_PER_KV):
            sink = jnp.where(lane_head == r, sink_ref[g * Q_PER_KV + r] * LOG2E, sink)
        m = jnp.maximum(jnp.max(s, axis=0, keepdims=True), sink)
        p = jnp.exp2(s - m).astype(bf16)
        ot = jnp.dot(vt_scr[g, :, j * BLOCK:j * BLOCK + BAND], p, preferred_element_type=f32)
        denom = ot[HEAD_DIM:HEAD_DIM + 1, :] + jnp.exp2(sink - m)
        out_t = ot[:HEAD_DIM, :] * (1.0 / denom)
        for c in range(Q_PER_KV // 2):
            pair = jnp.concatenate([out_t[:, (2 * c) * BLOCK:(2 * c + 1) * BLOCK],
                                    out_t[:, (2 * c + 1) * BLOCK:(2 * c + 2) * BLOCK]], axis=0)
            col = (g * Q_PER_KV + 2 * c) * HEAD_DIM
            o_ref[j * BLOCK:(j + 1) * BLOCK, col:col + 2 * HEAD_DIM] = pair.T.astype(bf16)

    steps = [(j, g) for j in range(n_qb) for g in range(N_KV_HEADS)]
    s_next = scores(*steps[0])
    for n, (j, g) in enumerate(steps):
        s = s_next
        if n + 1 < len(steps):
            s_next = scores(*steps[n + 1])
        finish(j, g, s)


def _attention(qkv, bias, sink, seq_len):
    t = qkv.shape[0]
    tq = TOKEN_TILE
    tiles_per_seq = seq_len // tq
    blocks_per_tile = tq // BLOCK
    n_blocks = t // BLOCK
    kv_col_block = Q_COLS // KV_COLS
    return pl.pallas_call(
        functools.partial(_attn_kernel, tiles_per_seq),
        out_shape=jax.ShapeDtypeStruct((t, Q_COLS), bf16),
        grid=(t // tq,),
        in_specs=[
            pl.BlockSpec((tq, QKV_COLS), lambda i: (i, 0)),
            pl.BlockSpec((BLOCK, KV_COLS),
                         lambda i: (jnp.maximum(i * blocks_per_tile - 1, 0), kv_col_block)),
            pl.BlockSpec((BLOCK, KV_COLS),
                         lambda i: (jnp.minimum((i + 1) * blocks_per_tile, n_blocks - 1), kv_col_block)),
            _resident((N_KV_HEADS, BAND, GROUP_ROWS)),
            pl.BlockSpec(memory_space=pltpu.SMEM),
        ],
        out_specs=pl.BlockSpec((tq, Q_COLS), lambda i: (i, 0)),
        scratch_shapes=[
            pltpu.VMEM((tq + 2 * BLOCK, K_COLS), bf16),
            pltpu.VMEM((N_KV_HEADS, LANES, tq + 2 * BLOCK), bf16),
        ],
        compiler_params=_params(),
        name="band_attention",
    )(qkv, qkv, qkv, bias, sink)


def _conv_kernel(tiles_per_seq, x_ref, prev_ref, next_ref, g_ref, wb_ref, wcu_ref, cw_ref, wout_ref,
                 o_ref):
    n_sub = x_ref.shape[0] // SUB_ROWS
    i = pl.program_id(0)
    first = (i % tiles_per_seq) == 0
    last = (i % tiles_per_seq) == tiles_per_seq - 1
    g = g_ref[...]
    cw = cw_ref[...]
    edge = F32_SUBLANES
    z_main, b_gate = {}, {}
    z_edge = {}

    def rows(s):
        return slice(s * SUB_ROWS, (s + 1) * SUB_ROWS)

    def gate(s):
        h = _rms_norm(x_ref[rows(s), :], g).astype(bf16)
        parts, lo = [h], 0
        if s == 0:
            parts, lo = [_rms_norm(prev_ref[...], g).astype(bf16), h], CONV_HALO
        if s == n_sub - 1:
            parts = parts + [_rms_norm(next_ref[...], g).astype(bf16)]
        h_ext = parts[0] if len(parts) == 1 else jnp.concatenate(parts, axis=0)
        cu = jnp.dot(h_ext, wcu_ref[...], preferred_element_type=f32)
        z = cu[:, :D_MODEL] * cu[:, D_MODEL:]
        z_main[s] = z[lo:lo + SUB_ROWS]
        if s == 0:
            z_edge["lo"] = jnp.where(first, 0.0, z[lo - edge:lo])
        if s == n_sub - 1:
            z_edge["hi"] = jnp.where(last, 0.0, z[lo + SUB_ROWS:lo + SUB_ROWS + edge])
        b_gate[s] = jnp.dot(h, wb_ref[...], preferred_element_type=f32)

    def output(s):
        z = z_main[s]
        below = z_edge["lo"] if s == 0 else z_main[s - 1][SUB_ROWS - edge:]
        above = z_edge["hi"] if s == n_sub - 1 else z_main[s + 1][:edge]
        z_ext = jnp.concatenate([below, z, above], axis=0)
        n_ext = SUB_ROWS + 2 * edge
        z_prev = pltpu.roll(z_ext, 1, axis=0)[edge:edge + SUB_ROWS]
        z_next = pltpu.roll(z_ext, n_ext - 1, axis=0)[edge:edge + SUB_ROWS]
        conv = z_prev * cw[0:1] + z * cw[1:2] + z_next * cw[2:3]
        y = jnp.dot((b_gate.pop(s) * conv).astype(bf16), wout_ref[...], preferred_element_type=f32)
        o_ref[rows(s), :] = x_ref[rows(s), :] + y
        z_main.pop(s - 1, None)

    gate(0)
    for s in range(1, n_sub):
        gate(s)
        output(s - 1)
    output(n_sub - 1)


def _short_conv(x, g, w_b, w_cu, conv_w, w_out, seq_len):
    t = x.shape[0]
    tm = TOKEN_TILE
    tiles_per_seq = seq_len // tm
    halo_per_tile = tm // CONV_HALO
    n_halo_blocks = t // CONV_HALO
    return pl.pallas_call(
        functools.partial(_conv_kernel, tiles_per_seq),
        out_shape=jax.ShapeDtypeStruct((t, D_MODEL), f32),
        grid=(t // tm,),
        in_specs=[
            pl.BlockSpec((tm, D_MODEL), lambda i: (i, 0)),
            pl.BlockSpec((CONV_HALO, D_MODEL), lambda i: (jnp.maximum(i * halo_per_tile - 1, 0), 0)),
            pl.BlockSpec((CONV_HALO, D_MODEL),
                         lambda i: (jnp.minimum((i + 1) * halo_per_tile, n_halo_blocks - 1), 0)),
            _resident((1, D_MODEL)),
            _resident((D_MODEL, D_MODEL)),
            _resident((D_MODEL, 2 * D_MODEL)),
            _resident((CONV_WIDTH, D_MODEL)),
            _resident((D_MODEL, D_MODEL)),
        ],
        out_specs=pl.BlockSpec((tm, D_MODEL), lambda i: (i, 0)),
        compiler_params=_params(),
        name="short_conv",
    )(x, x, x, g, w_b, w_cu, conv_w, w_out)


def _mlp_body(x_ref, att_ref, wo_ref, p_ref, gm_ref, gp_ref, gf_ref, up_ref, down_ref, wg_ref, wp_ref,
              o_ref, apply_final_norm):
    n_sub = x_ref.shape[0] // SUB_ROWS
    n_chunk = D_FF // FF_CHUNK
    state = {}

    def rows(s):
        return slice(s * SUB_ROWS, (s + 1) * SUB_ROWS)

    def open_(s):
        x = x_ref[rows(s), :]
        if att_ref is not None:
            x = x + jnp.dot(att_ref[rows(s), :], wo_ref[...], preferred_element_type=f32)
        state[s] = {"h": _rms_norm(x, gm_ref[...]).astype(bf16), "acc": x}

    def chunk(s, c):
        st = state[s]
        a = jnp.dot(st["h"], up_ref[:, c * FF_CHUNK:(c + 1) * FF_CHUNK], preferred_element_type=f32)
        a = jnp.square(jnp.maximum(a, 0.0)).astype(bf16)
        st["acc"] = st["acc"] + jnp.dot(a, down_ref[c * FF_CHUNK:(c + 1) * FF_CHUNK, :],
                                        preferred_element_type=f32)

    def close(s):
        x1 = state.pop(s)["acc"]
        z = jnp.dot(_rms_norm(x1, gp_ref[...]).astype(bf16), wg_ref[...], preferred_element_type=f32)
        gate = 1.0 / (1.0 + jnp.exp(-z))
        pp = jnp.dot(p_ref[rows(s), :].astype(bf16), wp_ref[...], preferred_element_type=f32)
        y = x1 + pp * gate
        if apply_final_norm:
            y = _rms_norm(y, gf_ref[...])
        o_ref[rows(s), :] = y

    open_(0)
    for s in range(n_sub):
        for c in range(n_chunk):
            if c == n_chunk - 1 and s + 1 < n_sub:
                open_(s + 1)
            chunk(s, c)
            if c == 0 and s > 0:
                close(s - 1)
    close(n_sub - 1)


def _mlp_kernel(apply_final_norm, x_ref, *refs):
    _mlp_body(x_ref, None, None, *refs, apply_final_norm=apply_final_norm)


def _attn_mlp_kernel(apply_final_norm, x_ref, att_ref, wo_ref, *refs):
    _mlp_body(x_ref, att_ref, wo_ref, *refs, apply_final_norm=apply_final_norm)


def _mlp_ple(x, att, w_o, p, layer, g_mlp, g_ple, g_final, w_up, w_down, w_gate, w_proj, apply_final_norm):
    t = x.shape[0]
    tm = TOKEN_TILE
    row_tile = pl.BlockSpec((tm, D_MODEL), lambda i: (i, 0))
    if att is None:
        body, lead_args, lead_specs = _mlp_kernel, (x,), [row_tile]
    else:
        body = _attn_mlp_kernel
        lead_args = (x, att, w_o)
        lead_specs = [row_tile, pl.BlockSpec((tm, Q_COLS), lambda i: (i, 0)), _resident((Q_COLS, D_MODEL))]
    return pl.pallas_call(
        functools.partial(body, apply_final_norm),
        out_shape=jax.ShapeDtypeStruct((t, D_MODEL), f32),
        grid=(t // tm,),
        in_specs=lead_specs + [
            pl.BlockSpec((None, tm, PLE_DIM), lambda i: (layer, i, 0)),
            _resident((1, D_MODEL)),
            _resident((1, D_MODEL)),
            _resident((1, D_MODEL)),
            _resident((D_MODEL, D_FF)),
            _resident((D_FF, D_MODEL)),
            _resident((D_MODEL, D_MODEL)),
            _resident((PLE_DIM, D_MODEL)),
        ],
        out_specs=row_tile,
        compiler_params=_params(),
        name="mlp_ple",
    )(*lead_args, p, g_mlp, g_ple, g_final, w_up, w_down, w_gate, w_proj)


def _trunk(x, p, band_bias, w):
    b, s, _ = x.shape
    t = b * s
    x = x.reshape(t, D_MODEL)
    p = p.reshape(DEPTH, t, PLE_DIM)
    for i in range(DEPTH):
        j = i // N_MIXERS
        g_mix = w["norm_mix"][i][None, :]
        att = w_o = None
        if i % N_MIXERS == 0:
            qkv = _qkv_proj(x, g_mix, w["attn_w_qkv"][j])
            att = _attention(qkv, band_bias, w["attn_sink"][j], s)
            w_o = w["attn_w_o"][j]
        else:
            x = _short_conv(x, g_mix, w["conv_w_b"][j], w["conv_w_cu"][j], w["conv_w"][j],
                            w["conv_w_out"][j], s)
        x = _mlp_ple(x, att, w_o, p, i, w["norm_mlp"][i][None, :], w["norm_ple"][i][None, :],
                     w["final_norm"][None, :], w["mlp_w_up"][i], w["mlp_w_down"][i],
                     w["ple_w_gate"][i], w["ple_w_proj"][i], apply_final_norm=(i == DEPTH - 1))
    return x.reshape(b, s, D_MODEL)


def kernel(x_prompt, x_sample, p_prompt, p_sample, rel_bias, attn_w_qkv, attn_w_o, attn_sink,
           conv_w_in, conv_w, conv_w_out, mlp_w_up, mlp_w_down, ple_w_gate, ple_w_proj,
           norm_mix, norm_mlp, norm_ple, final_norm):
    w = {
        "attn_w_qkv": attn_w_qkv.astype(bf16),
        "attn_w_o": attn_w_o.astype(bf16),
        "attn_sink": attn_sink.astype(f32),
        "conv_w_b": conv_w_in[:, :, :D_MODEL].astype(bf16),
        "conv_w_cu": conv_w_in[:, :, D_MODEL:].astype(bf16),
        "conv_w": conv_w.astype(f32),
        "conv_w_out": conv_w_out.astype(bf16),
        "mlp_w_up": mlp_w_up.astype(bf16),
        "mlp_w_down": mlp_w_down.astype(bf16),
        "ple_w_gate": ple_w_gate.astype(bf16),
        "ple_w_proj": ple_w_proj.astype(bf16),
        "norm_mix": norm_mix.astype(f32),
        "norm_mlp": norm_mlp.astype(f32),
        "norm_ple": norm_ple.astype(f32),
        "final_norm": final_norm.astype(f32),
    }
    band_bias = _band_bias(rel_bias)
    y_prompt = _trunk(x_prompt, p_prompt, band_bias, w)
    y_sample = _trunk(x_sample, p_sample, band_bias, w)
    return (y_prompt, y_sample)
```

```python
import functools
import math

import jax
import jax.numpy as jnp
from jax import lax
from jax.experimental import pallas as pl
from jax.experimental.pallas import tpu as pltpu

D_MODEL = 1024
DEPTH = 4
N_MIXERS = 2
HEAD_DIM = 64
N_HEADS = D_MODEL // HEAD_DIM
N_KV_HEADS = N_HEADS // 4
Q_PER_KV = N_HEADS // N_KV_HEADS
WINDOW = 128
BLOCK = 128
N_BUCKETS = 32
MAX_DISTANCE = 128
CONV_WIDTH = 3
D_FF = 4 * D_MODEL
PLE_DIM = 256
EPS = 1e-6
NEG_INF = -1e30
LOG2E = math.log2(math.e)

Q_COLS = N_HEADS * HEAD_DIM
K_COLS = N_KV_HEADS * HEAD_DIM
KV_COLS = 2 * K_COLS
QKV_COLS = Q_COLS + KV_COLS
QUERY_BLOCK = 128
GROUP_LANES = Q_PER_KV * QUERY_BLOCK
BAND = QUERY_BLOCK + 2 * WINDOW

VMEM_LIMIT_BYTES = 56 * 1024 * 1024
BF16_SUBLANES = 16
F32_SUBLANES = 8
LANES = 128

TOKEN_TILE = 1024
MIXER_TILE = 2048
SUB_ROWS = 256
FF_CHUNK = 1024
CONV_HALO = BF16_SUBLANES

bf16 = jnp.bfloat16
f32 = jnp.float32


def _rms_norm(x, g):
    return x * lax.rsqrt(jnp.mean(x * x, axis=-1, keepdims=True) + EPS) * g


def _resident(shape):
    return pl.BlockSpec(shape, lambda *_: (0,) * len(shape), pipeline_mode=pl.Buffered(1))


def _params(n_axes=1):
    return pltpu.CompilerParams(
        dimension_semantics=("arbitrary",) * n_axes,
        vmem_limit_bytes=VMEM_LIMIT_BYTES,
    )


def _t5_bucket(rel):
    half = N_BUCKETS // 2
    max_exact = half // 2
    ret = jnp.where(rel > 0, half, 0)
    n = jnp.abs(rel)
    nf = jnp.maximum(n, 1).astype(jnp.float32)
    large = max_exact + (jnp.log(nf / max_exact) / math.log(MAX_DISTANCE / max_exact)
                         * (half - max_exact)).astype(jnp.int32)
    large = jnp.minimum(large, half - 1)
    return ret + jnp.where(n < max_exact, n, large)


def _bias_kernel(bucket_ref, rel_bias_ref, out_ref):
    bucket = bucket_ref[...]
    for g in range(N_KV_HEADS):
        for r in range(Q_PER_KV):
            h = g * Q_PER_KV + r
            acc = jnp.full(bucket.shape, NEG_INF, f32)
            for b in range(N_BUCKETS):
                acc = jnp.where(bucket == b, rel_bias_ref[b, h] * LOG2E, acc)
            out_ref[g, :, r * QUERY_BLOCK:(r + 1) * QUERY_BLOCK] = acc


def _band_bias(rel_bias):
    rel = (jnp.arange(BAND)[:, None] - WINDOW) - jnp.arange(QUERY_BLOCK)[None, :]
    bucket = jnp.where(jnp.abs(rel) <= WINDOW, _t5_bucket(rel), -1).astype(jnp.int32)
    return pl.pallas_call(
        _bias_kernel,
        out_shape=jax.ShapeDtypeStruct((N_KV_HEADS, BAND, GROUP_LANES), f32),
        in_specs=[pl.BlockSpec(memory_space=pltpu.VMEM), pl.BlockSpec(memory_space=pltpu.SMEM)],
        out_specs=pl.BlockSpec(memory_space=pltpu.VMEM),
        name="band_bias",
    )(bucket, rel_bias.astype(f32))


def _qkv_kernel(x_ref, g_ref, w_ref, o_ref):
    for s in range(x_ref.shape[0] // SUB_ROWS):
        rows = slice(s * SUB_ROWS, (s + 1) * SUB_ROWS)
        h = _rms_norm(x_ref[rows, :], g_ref[...]).astype(bf16)
        qkv = jnp.dot(h, w_ref[...], preferred_element_type=f32)
        o_ref[rows, :Q_COLS] = (qkv[:, :Q_COLS] * (HEAD_DIM ** -0.5 * LOG2E)).astype(bf16)
        o_ref[rows, Q_COLS:] = qkv[:, Q_COLS:].astype(bf16)


def _qkv_proj(x, g, w):
    t = x.shape[0]
    return pl.pallas_call(
        _qkv_kernel,
        out_shape=jax.ShapeDtypeStruct((t, QKV_COLS), bf16),
        grid=(t // MIXER_TILE,),
        in_specs=[
            pl.BlockSpec((MIXER_TILE, D_MODEL), lambda i: (i, 0)),
            _resident((1, D_MODEL)),
            _resident((D_MODEL, QKV_COLS)),
        ],
        out_specs=pl.BlockSpec((MIXER_TILE, QKV_COLS), lambda i: (i, 0)),
        compiler_params=_params(),
        name="qkv_proj",
    )(x, g, w)


def _attn_kernel(tiles_per_seq, qkv_ref, prev_ref, next_ref, bias_ref, sink_ref, o_ref, k_scr, vt_scr):
    tq = qkv_ref.shape[0]
    n_qb = tq // QUERY_BLOCK
    n_ext = tq + 2 * WINDOW
    i = pl.program_id(0)
    first = (i % tiles_per_seq) == 0
    last = (i % tiles_per_seq) == tiles_per_seq - 1

    k_scr[0:WINDOW, :] = prev_ref[:, :K_COLS]
    k_scr[WINDOW:WINDOW + tq, :] = qkv_ref[:, Q_COLS:Q_COLS + K_COLS]
    k_scr[WINDOW + tq:, :] = next_ref[:, :K_COLS]
    v_all = jnp.concatenate(
        [prev_ref[:, K_COLS:], qkv_ref[:, Q_COLS + K_COLS:], next_ref[:, K_COLS:]], axis=0)
    v_t = v_all.T
    ones = jnp.ones((LANES - HEAD_DIM, n_ext), bf16)
    for a in range(LANES // QUERY_BLOCK):
        off = a * QUERY_BLOCK
        for g in range(N_KV_HEADS):
            vt_scr[a, g, 0:HEAD_DIM, 0:n_ext - off] = v_t[g * HEAD_DIM:(g + 1) * HEAD_DIM, off:]
            vt_scr[a, g, HEAD_DIM:, :] = ones

    lane_head = lax.broadcasted_iota(jnp.int32, (1, GROUP_LANES), 1) // QUERY_BLOCK

    def mask_rows(s, lo, hi, dead):
        if lo >= hi:
            return s
        parts = [s[:lo]] if lo else []
        parts.append(jnp.where(dead, NEG_INF, s[lo:hi]))
        if hi < BAND:
            parts.append(s[hi:])
        return jnp.concatenate(parts, axis=0)

    def scores(j, g):
        rows = slice(j * QUERY_BLOCK, (j + 1) * QUERY_BLOCK)
        q = jnp.concatenate(
            [qkv_ref[rows, (g * Q_PER_KV + r) * HEAD_DIM:(g * Q_PER_KV + r + 1) * HEAD_DIM]
             for r in range(Q_PER_KV)], axis=0)
        k = k_scr[j * QUERY_BLOCK:j * QUERY_BLOCK + BAND, g * HEAD_DIM:(g + 1) * HEAD_DIM]
        s = lax.dot_general(k, q, (((1,), (1,)), ((), ())),
                            preferred_element_type=f32) + bias_ref[g]
        s = mask_rows(s, 0, WINDOW - j * QUERY_BLOCK, first)
        s = mask_rows(s, BAND - (WINDOW - (n_qb - 1 - j) * QUERY_BLOCK), BAND, last)
        return s

    def softmax(g, s):
        sink = jnp.zeros((1, GROUP_LANES), f32)
        for r in range(Q_PER_KV):
            sink = jnp.where(lane_head == r, sink_ref[g * Q_PER_KV + r] * LOG2E, sink)
        m = jnp.maximum(jnp.max(s, axis=0, keepdims=True), sink)
        return jnp.exp2(s - m).astype(bf16), jnp.exp2(sink - m)

    def weighted_values(j, g, p, sink_p):
        a = j % (LANES // QUERY_BLOCK)
        lane0 = (j - a) * QUERY_BLOCK
        ot = jnp.dot(vt_scr[a, g, :, lane0:lane0 + BAND], p, preferred_element_type=f32)
        denom = ot[HEAD_DIM:HEAD_DIM + 1, :] + sink_p
        out_t = ot[:HEAD_DIM, :] * (1.0 / denom)
        for c in range(Q_PER_KV // 2):
            pair = jnp.concatenate([out_t[:, (2 * c) * QUERY_BLOCK:(2 * c + 1) * QUERY_BLOCK],
                                    out_t[:, (2 * c + 1) * QUERY_BLOCK:(2 * c + 2) * QUERY_BLOCK]], axis=0)
            col = (g * Q_PER_KV + 2 * c) * HEAD_DIM
            o_ref[j * QUERY_BLOCK:(j + 1) * QUERY_BLOCK, col:col + 2 * HEAD_DIM] = pair.T.astype(bf16)

    steps = [(j, g) for j in range(n_qb) for g in range(N_KV_HEADS)]
    s_next = scores(*steps[0])
    for n, (j, g) in enumerate(steps):
        s = s_next
        if n + 1 < len(steps):
            s_next = scores(*steps[n + 1])
        weighted_values(j, g, *softmax(g, s))


def _attention(qkv, bias, sink, seq_len):
    t = qkv.shape[0]
    tq = MIXER_TILE
    tiles_per_seq = seq_len // tq
    halo_per_tile = tq // WINDOW
    n_halo_blocks = t // WINDOW
    kv_col_block = Q_COLS // KV_COLS
    return pl.pallas_call(
        functools.partial(_attn_kernel, tiles_per_seq),
        out_shape=jax.ShapeDtypeStruct((t, Q_COLS), bf16),
        grid=(t // tq,),
        in_specs=[
            pl.BlockSpec((tq, QKV_COLS), lambda i: (i, 0)),
            pl.BlockSpec((WINDOW, KV_COLS),
                         lambda i: (jnp.maximum(i * halo_per_tile - 1, 0), kv_col_block)),
            pl.BlockSpec((WINDOW, KV_COLS),
                         lambda i: (jnp.minimum((i + 1) * halo_per_tile, n_halo_blocks - 1), kv_col_block)),
            _resident((N_KV_HEADS, BAND, GROUP_LANES)),
            pl.BlockSpec(memory_space=pltpu.SMEM),
        ],
        out_specs=pl.BlockSpec((tq, Q_COLS), lambda i: (i, 0)),
        scratch_shapes=[
            pltpu.VMEM((tq + 2 * WINDOW, K_COLS), bf16),
            pltpu.VMEM((LANES // QUERY_BLOCK, N_KV_HEADS, LANES, tq + 2 * WINDOW), bf16),
        ],
        compiler_params=_params(),
        name="band_attention",
    )(qkv, qkv, qkv, bias, sink)


def _conv_kernel(tiles_per_seq, x_ref, prev_ref, next_ref, g_ref, wb_ref, wcu_ref, cw_ref, wout_ref,
                 o_ref):
    n_sub = x_ref.shape[0] // SUB_ROWS
    i = pl.program_id(0)
    first = (i % tiles_per_seq) == 0
    last = (i % tiles_per_seq) == tiles_per_seq - 1
    g = g_ref[...]
    cw = cw_ref[...]
    edge = F32_SUBLANES
    z_main, b_gate = {}, {}
    z_edge = {}

    def rows(s):
        return slice(s * SUB_ROWS, (s + 1) * SUB_ROWS)

    def gate(s):
        h = _rms_norm(x_ref[rows(s), :], g).astype(bf16)
        parts, lo = [h], 0
        if s == 0:
            parts, lo = [_rms_norm(prev_ref[...], g).astype(bf16), h], CONV_HALO
        if s == n_sub - 1:
            parts = parts + [_rms_norm(next_ref[...], g).astype(bf16)]
        h_ext = parts[0] if len(parts) == 1 else jnp.concatenate(parts, axis=0)
        cu = jnp.dot(h_ext, wcu_ref[...], preferred_element_type=f32)
        z = cu[:, :D_MODEL] * cu[:, D_MODEL:]
        z_main[s] = z[lo:lo + SUB_ROWS]
        if s == 0:
            z_edge["lo"] = jnp.where(first, 0.0, z[lo - edge:lo])
        if s == n_sub - 1:
            z_edge["hi"] = jnp.where(last, 0.0, z[lo + SUB_ROWS:lo + SUB_ROWS + edge])
        b_gate[s] = jnp.dot(h, wb_ref[...], preferred_element_type=f32)

    def output(s):
        z = z_main[s]
        below = z_edge["lo"] if s == 0 else z_main[s - 1][SUB_ROWS - edge:]
        above = z_edge["hi"] if s == n_sub - 1 else z_main[s + 1][:edge]
        z_ext = jnp.concatenate([below, z, above], axis=0)
        n_ext = SUB_ROWS + 2 * edge
        z_prev = pltpu.roll(z_ext, 1, axis=0)[edge:edge + SUB_ROWS]
        z_next = pltpu.roll(z_ext, n_ext - 1, axis=0)[edge:edge + SUB_ROWS]
        conv = z_prev * cw[0:1] + z * cw[1:2] + z_next * cw[2:3]
        y = jnp.dot((b_gate.pop(s) * conv).astype(bf16), wout_ref[...], preferred_element_type=f32)
        o_ref[rows(s), :] = x_ref[rows(s), :] + y
        z_main.pop(s - 1, None)

    gate(0)
    for s in range(1, n_sub):
        gate(s)
        output(s - 1)
    output(n_sub - 1)


def _short_conv(x, g, w_b, w_cu, conv_w, w_out, seq_len):
    t = x.shape[0]
    tm = MIXER_TILE
    tiles_per_seq = seq_len // tm
    halo_per_tile = tm // CONV_HALO
    n_halo_blocks = t // CONV_HALO
    return pl.pallas_call(
        functools.partial(_conv_kernel, tiles_per_seq),
        out_shape=jax.ShapeDtypeStruct((t, D_MODEL), f32),
        grid=(t // tm,),
        in_specs=[
            pl.BlockSpec((tm, D_MODEL), lambda i: (i, 0)),
            pl.BlockSpec((CONV_HALO, D_MODEL), lambda i: (jnp.maximum(i * halo_per_tile - 1, 0), 0)),
            pl.BlockSpec((CONV_HALO, D_MODEL),
                         lambda i: (jnp.minimum((i + 1) * halo_per_tile, n_halo_blocks - 1), 0)),
            _resident((1, D_MODEL)),
            _resident((D_MODEL, D_MODEL)),
            _resident((D_MODEL, 2 * D_MODEL)),
            _resident((CONV_WIDTH, D_MODEL)),
            _resident((D_MODEL, D_MODEL)),
        ],
        out_specs=pl.BlockSpec((tm, D_MODEL), lambda i: (i, 0)),
        compiler_params=_params(),
        name="short_conv",
    )(x, x, x, g, w_b, w_cu, conv_w, w_out)


def _mlp_body(x_ref, att_ref, wo_ref, p_ref, gm_ref, gp_ref, gf_ref, up_ref, down_ref, wg_ref, wp_ref,
              o_ref, apply_final_norm):
    n_sub = x_ref.shape[0] // SUB_ROWS
    n_chunk = D_FF // FF_CHUNK
    state = {}

    def rows(s):
        return slice(s * SUB_ROWS, (s + 1) * SUB_ROWS)

    def open_(s):
        x = x_ref[rows(s), :]
        if att_ref is not None:
            x = x + jnp.dot(att_ref[rows(s), :], wo_ref[...], preferred_element_type=f32)
        state[s] = {"h": _rms_norm(x, gm_ref[...]).astype(bf16), "acc": x}

    def chunk(s, c):
        st = state[s]
        a = jnp.dot(st["h"], up_ref[:, c * FF_CHUNK:(c + 1) * FF_CHUNK], preferred_element_type=f32)
        a = jnp.square(jnp.maximum(a, 0.0)).astype(bf16)
        st["acc"] = st["acc"] + jnp.dot(a, down_ref[c * FF_CHUNK:(c + 1) * FF_CHUNK, :],
                                        preferred_element_type=f32)

    def close(s):
        x1 = state.pop(s)["acc"]
        z = jnp.dot(_rms_norm(x1, gp_ref[...]).astype(bf16), wg_ref[...], preferred_element_type=f32)
        gate = 1.0 / (1.0 + jnp.exp(-z))
        pp = jnp.dot(p_ref[rows(s), :].astype(bf16), wp_ref[...], preferred_element_type=f32)
        y = x1 + pp * gate
        if apply_final_norm:
            y = _rms_norm(y, gf_ref[...])
        o_ref[rows(s), :] = y

    open_(0)
    for s in range(n_sub):
        for c in range(n_chunk):
            if c == n_chunk - 1 and s + 1 < n_sub:
                open_(s + 1)
            chunk(s, c)
            if c == 0 and s > 0:
                close(s - 1)
    close(n_sub - 1)


def _mlp_kernel(apply_final_norm, x_ref, *refs):
    _mlp_body(x_ref, None, None, *refs, apply_final_norm=apply_final_norm)


def _attn_mlp_kernel(apply_final_norm, x_ref, att_ref, wo_ref, *refs):
    _mlp_body(x_ref, att_ref, wo_ref, *refs, apply_final_norm=apply_final_norm)


def _mlp_ple(x, att, w_o, p, layer, g_mlp, g_ple, g_final, w_up, w_down, w_gate, w_proj, apply_final_norm):
    t = x.shape[0]
    tm = TOKEN_TILE
    row_tile = pl.BlockSpec((tm, D_MODEL), lambda i: (i, 0))
    if att is None:
        body, lead_args, lead_specs = _mlp_kernel, (x,), [row_tile]
    else:
        body = _attn_mlp_kernel
        lead_args = (x, att, w_o)
        lead_specs = [row_tile, pl.BlockSpec((tm, Q_COLS), lambda i: (i, 0)), _resident((Q_COLS, D_MODEL))]
    return pl.pallas_call(
        functools.partial(body, apply_final_norm),
        out_shape=jax.ShapeDtypeStruct((t, D_MODEL), f32),
        grid=(t // tm,),
        in_specs=lead_specs + [
            pl.BlockSpec((None, tm, PLE_DIM), lambda i: (layer, i, 0)),
            _resident((1, D_MODEL)),
            _resident((1, D_MODEL)),
            _resident((1, D_MODEL)),
            _resident((D_MODEL, D_FF)),
            _resident((D_FF, D_MODEL)),
            _resident((D_MODEL, D_MODEL)),
            _resident((PLE_DIM, D_MODEL)),
        ],
        out_specs=row_tile,
        compiler_params=_params(),
        name="mlp_ple",
    )(*lead_args, p, g_mlp, g_ple, g_final, w_up, w_down, w_gate, w_proj)


def _trunk(x, p, band_bias, w):
    b, s, _ = x.shape
    t = b * s
    x = x.reshape(t, D_MODEL)
    p = p.reshape(DEPTH, t, PLE_DIM)
    for i in range(DEPTH):
        j = i // N_MIXERS
        g_mix = w["norm_mix"][i][None, :]
        att = w_o = None
        if i % N_MIXERS == 0:
            qkv = _qkv_proj(x, g_mix, w["attn_w_qkv"][j])
            att = _attention(qkv, band_bias, w["attn_sink"][j], s)
            w_o = w["attn_w_o"][j]
        else:
            x = _short_conv(x, g_mix, w["conv_w_b"][j], w["conv_w_cu"][j], w["conv_w"][j],
                            w["conv_w_out"][j], s)
        x = _mlp_ple(x, att, w_o, p, i, w["norm_mlp"][i][None, :], w["norm_ple"][i][None, :],
                     w["final_norm"][None, :], w["mlp_w_up"][i], w["mlp_w_down"][i],
                     w["ple_w_gate"][i], w["ple_w_proj"][i], apply_final_norm=(i == DEPTH - 1))
    return x.reshape(b, s, D_MODEL)


def kernel(x_prompt, x_sample, p_prompt, p_sample, rel_bias, attn_w_qkv, attn_w_o, attn_sink,
           conv_w_in, conv_w, conv_w_out, mlp_w_up, mlp_w_down, ple_w_gate, ple_w_proj,
           norm_mix, norm_mlp, norm_ple, final_norm):
    w = {
        "attn_w_qkv": attn_w_qkv.astype(bf16),
        "attn_w_o": attn_w_o.astype(bf16),
        "attn_sink": attn_sink.astype(f32),
        "conv_w_b": conv_w_in[:, :, :D_MODEL].astype(bf16),
        "conv_w_cu": conv_w_in[:, :, D_MODEL:].astype(bf16),
        "conv_w": conv_w.astype(f32),
        "conv_w_out": conv_w_out.astype(bf16),
        "mlp_w_up": mlp_w_up.astype(bf16),
        "mlp_w_down": mlp_w_down.astype(bf16),
        "ple_w_gate": ple_w_gate.astype(bf16),
        "ple_w_proj": ple_w_proj.astype(bf16),
        "norm_mix": norm_mix.astype(f32),
        "norm_mlp": norm_mlp.astype(f32),
        "norm_ple": norm_ple.astype(f32),
        "final_norm": final_norm.astype(f32),
    }
    band_bias = _band_bias(rel_bias)
    y_prompt = _trunk(x_prompt, p_prompt, band_bias, w)
    y_sample = _trunk(x_sample, p_sample, band_bias, w)
    return (y_prompt, y_sample)
```

```python
import functools
import math

import jax
import jax.numpy as jnp
from jax import lax
from jax.experimental import pallas as pl
from jax.experimental.pallas import tpu as pltpu

D_MODEL = 1024
DEPTH = 4
N_MIXERS = 2
HEAD_DIM = 64
N_HEADS = D_MODEL // HEAD_DIM
N_KV_HEADS = N_HEADS // 4
Q_PER_KV = N_HEADS // N_KV_HEADS
WINDOW = 128
BLOCK = 128
N_BUCKETS = 32
MAX_DISTANCE = 128
CONV_WIDTH = 3
D_FF = 4 * D_MODEL
PLE_DIM = 256
EPS = 1e-6
NEG_INF = -1e30
LOG2E = math.log2(math.e)

Q_COLS = N_HEADS * HEAD_DIM
K_COLS = N_KV_HEADS * HEAD_DIM
KV_COLS = 2 * K_COLS
QKV_COLS = Q_COLS + KV_COLS
QUERY_BLOCK = 128
GROUP_LANES = Q_PER_KV * QUERY_BLOCK
BAND = QUERY_BLOCK + 2 * WINDOW

VMEM_LIMIT_BYTES = 56 * 1024 * 1024
BF16_SUBLANES = 16
F32_SUBLANES = 8
LANES = 128

TOKEN_TILE = 1024
MIXER_TILE = 2048
SUB_ROWS = 256
FF_CHUNK = 1024
CONV_HALO = BF16_SUBLANES

bf16 = jnp.bfloat16
f32 = jnp.float32


def _rms_norm(x, g):
    return x * lax.rsqrt(jnp.mean(x * x, axis=-1, keepdims=True) + EPS) * g


def _resident(shape, layer=None):
    if layer is None:
        return pl.BlockSpec(shape, lambda *_: (0,) * len(shape), pipeline_mode=pl.Buffered(1))
    return pl.BlockSpec((None,) + shape, lambda *_: (layer,) + (0,) * len(shape),
                        pipeline_mode=pl.Buffered(1))


def _params(n_axes=1):
    return pltpu.CompilerParams(
        dimension_semantics=("arbitrary",) * n_axes,
        vmem_limit_bytes=VMEM_LIMIT_BYTES,
    )


def _t5_bucket(rel):
    half = N_BUCKETS // 2
    max_exact = half // 2
    ret = jnp.where(rel > 0, half, 0)
    n = jnp.abs(rel)
    nf = jnp.maximum(n, 1).astype(jnp.float32)
    large = max_exact + (jnp.log(nf / max_exact) / math.log(MAX_DISTANCE / max_exact)
                         * (half - max_exact)).astype(jnp.int32)
    large = jnp.minimum(large, half - 1)
    return ret + jnp.where(n < max_exact, n, large)


def _bias_kernel(bucket_ref, rel_bias_ref, out_ref):
    bucket = bucket_ref[...]
    for g in range(N_KV_HEADS):
        for r in range(Q_PER_KV):
            h = g * Q_PER_KV + r
            acc = jnp.full(bucket.shape, NEG_INF, f32)
            for b in range(N_BUCKETS):
                acc = jnp.where(bucket == b, rel_bias_ref[b, h] * LOG2E, acc)
            out_ref[g, :, r * QUERY_BLOCK:(r + 1) * QUERY_BLOCK] = acc


def _band_bias(rel_bias):
    rel = (jnp.arange(BAND)[:, None] - WINDOW) - jnp.arange(QUERY_BLOCK)[None, :]
    bucket = jnp.where(jnp.abs(rel) <= WINDOW, _t5_bucket(rel), -1).astype(jnp.int32)
    return pl.pallas_call(
        _bias_kernel,
        out_shape=jax.ShapeDtypeStruct((N_KV_HEADS, BAND, GROUP_LANES), f32),
        in_specs=[pl.BlockSpec(memory_space=pltpu.VMEM), pl.BlockSpec(memory_space=pltpu.SMEM)],
        out_specs=pl.BlockSpec(memory_space=pltpu.VMEM),
        name="band_bias",
    )(bucket, rel_bias.astype(f32))


PROJ_COLS = 256


def _attn_kernel(tiles_per_seq, mixer, x_ref, xprev_ref, xnext_ref, gain_ref, w_ref, bias_ref, sink_ref,
                 o_ref, q_scr, k_scr, vt_scr):
    tq = x_ref.shape[0]
    n_sub = tq // SUB_ROWS
    n_qb = tq // QUERY_BLOCK
    n_ext = tq + 2 * WINDOW
    i = pl.program_id(0)
    first = (i % tiles_per_seq) == 0
    last = (i % tiles_per_seq) == tiles_per_seq - 1
    gain = gain_ref[...]

    ones = jnp.ones((LANES - HEAD_DIM, n_ext), bf16)
    for g in range(N_KV_HEADS):
        vt_scr[g, HEAD_DIM:, :] = ones

    normed = {}

    def normed_rows(key):
        if key not in normed:
            src = {"prev": xprev_ref, "next": xnext_ref}.get(key)
            rows = src[...] if src is not None else x_ref[key * SUB_ROWS:(key + 1) * SUB_ROWS, :]
            normed[key] = _rms_norm(rows, gain).astype(bf16)
        return normed[key]

    def ext_row0(key):
        if key == "prev":
            return 0
        if key == "next":
            return WINDOW + tq
        return WINDOW + key * SUB_ROWS

    def project(key, c):
        h = normed_rows(key)
        y = jnp.dot(h, w_ref[:, c * PROJ_COLS:(c + 1) * PROJ_COLS], preferred_element_type=f32)
        n_rows = h.shape[0]
        if c < Q_COLS // PROJ_COLS:
            q_scr[key * SUB_ROWS:(key + 1) * SUB_ROWS, c * PROJ_COLS:(c + 1) * PROJ_COLS] = (
                y * (HEAD_DIM ** -0.5 * LOG2E)).astype(bf16)
        elif c == Q_COLS // PROJ_COLS:
            k_scr[ext_row0(key):ext_row0(key) + n_rows, :] = y.astype(bf16)
        else:
            v_t = y.astype(bf16).T
            for g in range(N_KV_HEADS):
                vt_scr[g, 0:HEAD_DIM, ext_row0(key):ext_row0(key) + n_rows] = (
                    v_t[g * HEAD_DIM:(g + 1) * HEAD_DIM, :])

    n_pieces = QKV_COLS // PROJ_COLS
    kv_pieces = list(range(Q_COLS // PROJ_COLS, n_pieces))

    lane_head = lax.broadcasted_iota(jnp.int32, (1, GROUP_LANES), 1) // QUERY_BLOCK

    def mask_rows(s, lo, hi, dead):
        if lo >= hi:
            return s
        parts = [s[:lo]] if lo else []
        parts.append(jnp.where(dead, NEG_INF, s[lo:hi]))
        if hi < BAND:
            parts.append(s[hi:])
        return jnp.concatenate(parts, axis=0)

    def scores(j, g):
        rows = slice(j * QUERY_BLOCK, (j + 1) * QUERY_BLOCK)
        q = jnp.concatenate(
            [q_scr[rows, (g * Q_PER_KV + r) * HEAD_DIM:(g * Q_PER_KV + r + 1) * HEAD_DIM]
             for r in range(Q_PER_KV)], axis=0)
        k = k_scr[j * QUERY_BLOCK:j * QUERY_BLOCK + BAND, g * HEAD_DIM:(g + 1) * HEAD_DIM]
        s = lax.dot_general(k, q, (((1,), (1,)), ((), ())),
                            preferred_element_type=f32) + bias_ref[g]
        s = mask_rows(s, 0, WINDOW - j * QUERY_BLOCK, first)
        s = mask_rows(s, BAND - (WINDOW - (n_qb - 1 - j) * QUERY_BLOCK), BAND, last)
        return s

    def softmax(g, s):
        sink = jnp.zeros((1, GROUP_LANES), f32)
        for r in range(Q_PER_KV):
            sink = jnp.where(lane_head == r, sink_ref[mixer, g * Q_PER_KV + r] * LOG2E, sink)
        m = jnp.maximum(jnp.max(s, axis=0, keepdims=True), sink)
        return jnp.exp2(s - m).astype(bf16), jnp.exp2(sink - m)

    def weighted_values(j, g, p, sink_p):
        ot = jnp.dot(vt_scr[g, :, j * QUERY_BLOCK:j * QUERY_BLOCK + BAND], p, preferred_element_type=f32)
        denom = ot[HEAD_DIM:HEAD_DIM + 1, :] + sink_p
        out_t = ot[:HEAD_DIM, :] * (1.0 / denom)
        for c in range(Q_PER_KV // 2):
            pair = jnp.concatenate([out_t[:, (2 * c) * QUERY_BLOCK:(2 * c + 1) * QUERY_BLOCK],
                                    out_t[:, (2 * c + 1) * QUERY_BLOCK:(2 * c + 2) * QUERY_BLOCK]], axis=0)
            col = (g * Q_PER_KV + 2 * c) * HEAD_DIM
            o_ref[j * QUERY_BLOCK:(j + 1) * QUERY_BLOCK, col:col + 2 * HEAD_DIM] = pair.T.astype(bf16)

    for c in kv_pieces:
        project("prev", c)
    for u in range(min(2, n_sub)):
        for c in range(n_pieces):
            project(u, c)
    qb_per_sub = SUB_ROWS // QUERY_BLOCK
    steps = [(j, g) for j in range(n_qb) for g in range(N_KV_HEADS)]
    s_next = scores(*steps[0])
    for n, (j, g) in enumerate(steps):
        s = s_next
        if n + 1 < len(steps):
            s_next = scores(*steps[n + 1])
        u = j // qb_per_sub
        piece = (j % qb_per_sub) * N_KV_HEADS + g
        if u + 2 < n_sub:
            if piece < n_pieces:
                project(u + 2, piece)
        elif u + 2 == n_sub and piece < len(kv_pieces):
            project("next", kv_pieces[piece])
        weighted_values(j, g, *softmax(g, s))


def _attention(x, gain, w_qkv, bias, sink, layer, mixer, seq_len):
    assert QUERY_BLOCK == LANES and SUB_ROWS % QUERY_BLOCK == 0
    assert (SUB_ROWS // QUERY_BLOCK) * N_KV_HEADS >= QKV_COLS // PROJ_COLS
    t = x.shape[0]
    tq = MIXER_TILE
    tiles_per_seq = seq_len // tq
    halo_per_tile = tq // WINDOW
    n_halo_blocks = t // WINDOW
    return pl.pallas_call(
        functools.partial(_attn_kernel, tiles_per_seq, mixer),
        out_shape=jax.ShapeDtypeStruct((t, Q_COLS), bf16),
        grid=(t // tq,),
        in_specs=[
            pl.BlockSpec((tq, D_MODEL), lambda i: (i, 0)),
            pl.BlockSpec((WINDOW, D_MODEL), lambda i: (jnp.maximum(i * halo_per_tile - 1, 0), 0)),
            pl.BlockSpec((WINDOW, D_MODEL),
                         lambda i: (jnp.minimum((i + 1) * halo_per_tile, n_halo_blocks - 1), 0)),
            _resident((1, D_MODEL), layer),
            _resident((D_MODEL, QKV_COLS), mixer),
            _resident((N_KV_HEADS, BAND, GROUP_LANES)),
            pl.BlockSpec(memory_space=pltpu.SMEM),
        ],
        out_specs=pl.BlockSpec((tq, Q_COLS), lambda i: (i, 0)),
        scratch_shapes=[
            pltpu.VMEM((tq, Q_COLS), bf16),
            pltpu.VMEM((tq + 2 * WINDOW, K_COLS), bf16),
            pltpu.VMEM((N_KV_HEADS, LANES, tq + 2 * WINDOW), bf16),
        ],
        compiler_params=_params(),
        name="band_attention",
    )(x, x, x, gain, w_qkv, bias, sink)


def _conv_kernel(tiles_per_seq, x_ref, prev_ref, next_ref, g_ref, win_ref, cw_ref, wout_ref, o_ref):
    n_sub = x_ref.shape[0] // SUB_ROWS
    i = pl.program_id(0)
    first = (i % tiles_per_seq) == 0
    last = (i % tiles_per_seq) == tiles_per_seq - 1
    g = g_ref[...]
    cw = cw_ref[...]
    edge = F32_SUBLANES
    z_main, b_gate = {}, {}
    z_edge = {}

    def rows(s):
        return slice(s * SUB_ROWS, (s + 1) * SUB_ROWS)

    def gate(s):
        h = _rms_norm(x_ref[rows(s), :], g).astype(bf16)
        parts, lo = [h], 0
        if s == 0:
            parts, lo = [_rms_norm(prev_ref[...], g).astype(bf16), h], CONV_HALO
        if s == n_sub - 1:
            parts = parts + [_rms_norm(next_ref[...], g).astype(bf16)]
        h_ext = parts[0] if len(parts) == 1 else jnp.concatenate(parts, axis=0)
        cu = jnp.dot(h_ext, win_ref[:, D_MODEL:], preferred_element_type=f32)
        z = cu[:, :D_MODEL] * cu[:, D_MODEL:]
        z_main[s] = z[lo:lo + SUB_ROWS]
        if s == 0:
            z_edge["lo"] = jnp.where(first, 0.0, z[lo - edge:lo])
        if s == n_sub - 1:
            z_edge["hi"] = jnp.where(last, 0.0, z[lo + SUB_ROWS:lo + SUB_ROWS + edge])
        b_gate[s] = jnp.dot(h, win_ref[:, :D_MODEL], preferred_element_type=f32)

    def output(s):
        z = z_main[s]
        below = z_edge["lo"] if s == 0 else z_main[s - 1][SUB_ROWS - edge:]
        above = z_edge["hi"] if s == n_sub - 1 else z_main[s + 1][:edge]
        z_ext = jnp.concatenate([below, z, above], axis=0)
        n_ext = SUB_ROWS + 2 * edge
        z_prev = pltpu.roll(z_ext, 1, axis=0)[edge:edge + SUB_ROWS]
        z_next = pltpu.roll(z_ext, n_ext - 1, axis=0)[edge:edge + SUB_ROWS]
        conv = z_prev * cw[0:1] + z * cw[1:2] + z_next * cw[2:3]
        y = jnp.dot((b_gate.pop(s) * conv).astype(bf16), wout_ref[...], preferred_element_type=f32)
        o_ref[rows(s), :] = x_ref[rows(s), :] + y
        z_main.pop(s - 1, None)

    gate(0)
    for s in range(1, n_sub):
        gate(s)
        output(s - 1)
    output(n_sub - 1)


def _short_conv(x, g, w_in, conv_w, w_out, layer, mixer, seq_len):
    t = x.shape[0]
    tm = MIXER_TILE
    tiles_per_seq = seq_len // tm
    halo_per_tile = tm // CONV_HALO
    n_halo_blocks = t // CONV_HALO
    return pl.pallas_call(
        functools.partial(_conv_kernel, tiles_per_seq),
        out_shape=jax.ShapeDtypeStruct((t, D_MODEL), f32),
        grid=(t // tm,),
        in_specs=[
            pl.BlockSpec((tm, D_MODEL), lambda i: (i, 0)),
            pl.BlockSpec((CONV_HALO, D_MODEL), lambda i: (jnp.maximum(i * halo_per_tile - 1, 0), 0)),
            pl.BlockSpec((CONV_HALO, D_MODEL),
                         lambda i: (jnp.minimum((i + 1) * halo_per_tile, n_halo_blocks - 1), 0)),
            _resident((1, D_MODEL), layer),
            _resident((D_MODEL, 3 * D_MODEL), mixer),
            _resident((CONV_WIDTH, D_MODEL), mixer),
            _resident((D_MODEL, D_MODEL), mixer),
        ],
        out_specs=pl.BlockSpec((tm, D_MODEL), lambda i: (i, 0)),
        compiler_params=_params(),
        name="short_conv",
    )(x, x, x, g, w_in, conv_w, w_out)


def _mlp_body(x_ref, att_ref, wo_ref, p_ref, gm_ref, gp_ref, gf_ref, up_ref, down_ref, wg_ref, wp_ref,
              o_ref, apply_final_norm):
    n_sub = x_ref.shape[0] // SUB_ROWS
    n_chunk = D_FF // FF_CHUNK
    state = {}

    def rows(s):
        return slice(s * SUB_ROWS, (s + 1) * SUB_ROWS)

    def open_(s):
        x = x_ref[rows(s), :]
        if att_ref is not None:
            x = x + jnp.dot(att_ref[rows(s), :], wo_ref[...], preferred_element_type=f32)
        state[s] = {"h": _rms_norm(x, gm_ref[...]).astype(bf16), "acc": x}

    def chunk(s, c):
        st = state[s]
        a = jnp.dot(st["h"], up_ref[:, c * FF_CHUNK:(c + 1) * FF_CHUNK], preferred_element_type=f32)
        a = jnp.square(jnp.maximum(a, 0.0)).astype(bf16)
        st["acc"] = st["acc"] + jnp.dot(a, down_ref[c * FF_CHUNK:(c + 1) * FF_CHUNK, :],
                                        preferred_element_type=f32)

    def close(s):
        x1 = state.pop(s)["acc"]
        z = jnp.dot(_rms_norm(x1, gp_ref[...]).astype(bf16), wg_ref[...], preferred_element_type=f32)
        gate = 1.0 / (1.0 + jnp.exp(-z))
        pp = jnp.dot(p_ref[rows(s), :].astype(bf16), wp_ref[...], preferred_element_type=f32)
        y = x1 + pp * gate
        if apply_final_norm:
            y = _rms_norm(y, gf_ref[...])
        o_ref[rows(s), :] = y

    open_(0)
    for s in range(n_sub):
        for c in range(n_chunk):
            if c == n_chunk - 1 and s + 1 < n_sub:
                open_(s + 1)
            chunk(s, c)
            if c == 0 and s > 0:
                close(s - 1)
    close(n_sub - 1)


def _mlp_kernel(apply_final_norm, x_ref, *refs):
    _mlp_body(x_ref, None, None, *refs, apply_final_norm=apply_final_norm)


def _attn_mlp_kernel(apply_final_norm, x_ref, att_ref, wo_ref, *refs):
    _mlp_body(x_ref, att_ref, wo_ref, *refs, apply_final_norm=apply_final_norm)


def _mlp_ple(x, att, w_o, mixer, p, layer, g_mlp, g_ple, g_final, w_up, w_down, w_gate, w_proj,
             apply_final_norm):
    t = x.shape[0]
    tm = TOKEN_TILE
    row_tile = pl.BlockSpec((tm, D_MODEL), lambda i: (i, 0))
    if att is None:
        body, lead_args, lead_specs = _mlp_kernel, (x,), [row_tile]
    else:
        body = _attn_mlp_kernel
        lead_args = (x, att, w_o)
        lead_specs = [row_tile, pl.BlockSpec((tm, Q_COLS), lambda i: (i, 0)),
                      _resident((Q_COLS, D_MODEL), mixer)]
    return pl.pallas_call(
        functools.partial(body, apply_final_norm),
        out_shape=jax.ShapeDtypeStruct((t, D_MODEL), f32),
        grid=(t // tm,),
        in_specs=lead_specs + [
            pl.BlockSpec((None, tm, PLE_DIM), lambda i: (layer, i, 0)),
            _resident((1, D_MODEL), layer),
            _resident((1, D_MODEL), layer),
            _resident((1, D_MODEL)),
            _resident((D_MODEL, D_FF), layer),
            _resident((D_FF, D_MODEL), layer),
            _resident((D_MODEL, D_MODEL), layer),
            _resident((PLE_DIM, D_MODEL), layer),
        ],
        out_specs=row_tile,
        compiler_params=_params(),
        name="mlp_ple",
    )(*lead_args, p, g_mlp, g_ple, g_final, w_up, w_down, w_gate, w_proj)


def _trunk(x, p, band_bias, w):
    b, s, _ = x.shape
    t = b * s
    x = x.reshape(t, D_MODEL)
    p = p.reshape(DEPTH, t, PLE_DIM)
    for i in range(DEPTH):
        j = i // N_MIXERS
        att = None
        if i % N_MIXERS == 0:
            att = _attention(x, w["norm_mix"], w["attn_w_qkv"], band_bias, w["attn_sink"], i, j, s)
        else:
            x = _short_conv(x, w["norm_mix"], w["conv_w_in"], w["conv_w"], w["conv_w_out"], i, j, s)
        x = _mlp_ple(x, att, w["attn_w_o"], j, p, i, w["norm_mlp"], w["norm_ple"], w["final_norm"],
                     w["mlp_w_up"], w["mlp_w_down"], w["ple_w_gate"], w["ple_w_proj"],
                     apply_final_norm=(i == DEPTH - 1))
    return x.reshape(b, s, D_MODEL)


def kernel(x_prompt, x_sample, p_prompt, p_sample, rel_bias, attn_w_qkv, attn_w_o, attn_sink,
           conv_w_in, conv_w, conv_w_out, mlp_w_up, mlp_w_down, ple_w_gate, ple_w_proj,
           norm_mix, norm_mlp, norm_ple, final_norm):
    w = {
        "attn_w_qkv": attn_w_qkv.astype(bf16),
        "attn_w_o": attn_w_o.astype(bf16),
        "attn_sink": attn_sink.astype(f32),
        "conv_w_in": conv_w_in.astype(bf16),
        "conv_w": conv_w.astype(f32),
        "conv_w_out": conv_w_out.astype(bf16),
        "mlp_w_up": mlp_w_up.astype(bf16),
        "mlp_w_down": mlp_w_down.astype(bf16),
        "ple_w_gate": ple_w_gate.astype(bf16),
        "ple_w_proj": ple_w_proj.astype(bf16),
        "norm_mix": norm_mix.astype(f32)[:, None, :],
        "norm_mlp": norm_mlp.astype(f32)[:, None, :],
        "norm_ple": norm_ple.astype(f32)[:, None, :],
        "final_norm": final_norm.astype(f32)[None, :],
    }
    band_bias = _band_bias(rel_bias)
    y_prompt = _trunk(x_prompt, p_prompt, band_bias, w)
    y_sample = _trunk(x_sample, p_sample, band_bias, w)
    return (y_prompt, y_sample)
```

```python
import functools
import math

import jax
import jax.numpy as jnp
from jax import lax
from jax.experimental import pallas as pl
from jax.experimental.pallas import tpu as pltpu

D_MODEL = 1024
DEPTH = 4
N_MIXERS = 2
HEAD_DIM = 64
N_HEADS = D_MODEL // HEAD_DIM
N_KV_HEADS = N_HEADS // 4
Q_PER_KV = N_HEADS // N_KV_HEADS
WINDOW = 128
N_BUCKETS = 32
MAX_DISTANCE = 128
CONV_WIDTH = 3
D_FF = 4 * D_MODEL
PLE_DIM = 256
EPS = 1e-6
NEG_INF = -1e30
LOG2E = math.log2(math.e)

Q_COLS = N_HEADS * HEAD_DIM
K_COLS = N_KV_HEADS * HEAD_DIM
KV_COLS = 2 * K_COLS
QKV_COLS = Q_COLS + KV_COLS
QUERY_BLOCK = 128
GROUP_LANES = Q_PER_KV * QUERY_BLOCK
BAND = QUERY_BLOCK + 2 * WINDOW

VMEM_LIMIT_BYTES = 56 * 1024 * 1024
BF16_SUBLANES = 16
F32_SUBLANES = 8
LANES = 128

TOKEN_TILE = 1024
MIXER_TILE = 2048
SUB_ROWS = 256
FF_CHUNK = 1024
CONV_HALO = BF16_SUBLANES

bf16 = jnp.bfloat16
f32 = jnp.float32


def _rms_norm(x, g):
    return x * lax.rsqrt(jnp.mean(x * x, axis=-1, keepdims=True) + EPS) * g


def _resident(shape, layer=None):
    if layer is None:
        return pl.BlockSpec(shape, lambda *_: (0,) * len(shape), pipeline_mode=pl.Buffered(1))
    return pl.BlockSpec((None,) + shape, lambda *_: (layer,) + (0,) * len(shape),
                        pipeline_mode=pl.Buffered(1))


def _params(n_axes=1):
    return pltpu.CompilerParams(
        dimension_semantics=("arbitrary",) * n_axes,
        vmem_limit_bytes=VMEM_LIMIT_BYTES,
    )


def _t5_bucket(rel):
    half = N_BUCKETS // 2
    max_exact = half // 2
    ret = jnp.where(rel > 0, half, 0)
    n = jnp.abs(rel)
    nf = jnp.maximum(n, 1).astype(jnp.float32)
    large = max_exact + (jnp.log(nf / max_exact) / math.log(MAX_DISTANCE / max_exact)
                         * (half - max_exact)).astype(jnp.int32)
    large = jnp.minimum(large, half - 1)
    return ret + jnp.where(n < max_exact, n, large)


def _bias_kernel(bucket_ref, rel_bias_ref, out_ref):
    bucket = bucket_ref[...]
    for g in range(N_KV_HEADS):
        for r in range(Q_PER_KV):
            h = g * Q_PER_KV + r
            acc = jnp.full(bucket.shape, NEG_INF, f32)
            for b in range(N_BUCKETS):
                acc = jnp.where(bucket == b, rel_bias_ref[b, h] * LOG2E, acc)
            out_ref[g, :, r * QUERY_BLOCK:(r + 1) * QUERY_BLOCK] = acc


def _band_bias(rel_bias):
    rel = (jnp.arange(BAND)[:, None] - WINDOW) - jnp.arange(QUERY_BLOCK)[None, :]
    bucket = jnp.where(jnp.abs(rel) <= WINDOW, _t5_bucket(rel), -1).astype(jnp.int32)
    return pl.pallas_call(
        _bias_kernel,
        out_shape=jax.ShapeDtypeStruct((N_KV_HEADS, BAND, GROUP_LANES), f32),
        in_specs=[pl.BlockSpec(memory_space=pltpu.VMEM), pl.BlockSpec(memory_space=pltpu.SMEM)],
        out_specs=pl.BlockSpec(memory_space=pltpu.VMEM),
        name="band_bias",
    )(bucket, rel_bias.astype(f32))


def _qkv_kernel(x_ref, g_ref, w_ref, o_ref):
    for s in range(x_ref.shape[0] // SUB_ROWS):
        rows = slice(s * SUB_ROWS, (s + 1) * SUB_ROWS)
        h = _rms_norm(x_ref[rows, :], g_ref[...]).astype(bf16)
        qkv = jnp.dot(h, w_ref[...], preferred_element_type=f32)
        o_ref[rows, :Q_COLS] = (qkv[:, :Q_COLS] * (HEAD_DIM ** -0.5 * LOG2E)).astype(bf16)
        o_ref[rows, Q_COLS:] = qkv[:, Q_COLS:].astype(bf16)


def _qkv_proj(x, g, w, layer, mixer):
    t = x.shape[0]
    return pl.pallas_call(
        _qkv_kernel,
        out_shape=jax.ShapeDtypeStruct((t, QKV_COLS), bf16),
        grid=(t // MIXER_TILE,),
        in_specs=[
            pl.BlockSpec((MIXER_TILE, D_MODEL), lambda i: (i, 0)),
            _resident((1, D_MODEL), layer),
            _resident((D_MODEL, QKV_COLS), mixer),
        ],
        out_specs=pl.BlockSpec((MIXER_TILE, QKV_COLS), lambda i: (i, 0)),
        compiler_params=_params(),
        name="qkv_proj",
    )(x, g, w)


def _attn_kernel(tiles_per_seq, mixer, qkv_ref, prev_ref, next_ref, bias_ref, sink_ref, o_ref, k_scr, vt_scr):
    tq = qkv_ref.shape[0]
    n_qb = tq // QUERY_BLOCK
    n_ext = tq + 2 * WINDOW
    i = pl.program_id(0)
    first = (i % tiles_per_seq) == 0
    last = (i % tiles_per_seq) == tiles_per_seq - 1

    k_scr[0:WINDOW, :] = prev_ref[:, :K_COLS]
    k_scr[WINDOW:WINDOW + tq, :] = qkv_ref[:, Q_COLS:Q_COLS + K_COLS]
    k_scr[WINDOW + tq:, :] = next_ref[:, :K_COLS]
    v_all = jnp.concatenate(
        [prev_ref[:, K_COLS:], qkv_ref[:, Q_COLS + K_COLS:], next_ref[:, K_COLS:]], axis=0)
    v_t = v_all.T
    ones = jnp.ones((LANES - HEAD_DIM, n_ext), bf16)
    for g in range(N_KV_HEADS):
        vt_scr[g, 0:HEAD_DIM, :] = v_t[g * HEAD_DIM:(g + 1) * HEAD_DIM, :]
        vt_scr[g, HEAD_DIM:, :] = ones

    lane_head = lax.broadcasted_iota(jnp.int32, (1, GROUP_LANES), 1) // QUERY_BLOCK

    def mask_rows(s, lo, hi, dead):
        if lo >= hi:
            return s
        parts = [s[:lo]] if lo else []
        parts.append(jnp.where(dead, NEG_INF, s[lo:hi]))
        if hi < BAND:
            parts.append(s[hi:])
        return jnp.concatenate(parts, axis=0)

    def scores(j, g):
        rows = slice(j * QUERY_BLOCK, (j + 1) * QUERY_BLOCK)
        q = jnp.concatenate(
            [qkv_ref[rows, (g * Q_PER_KV + r) * HEAD_DIM:(g * Q_PER_KV + r + 1) * HEAD_DIM]
             for r in range(Q_PER_KV)], axis=0)
        k = k_scr[j * QUERY_BLOCK:j * QUERY_BLOCK + BAND, g * HEAD_DIM:(g + 1) * HEAD_DIM]
        s = lax.dot_general(k, q, (((1,), (1,)), ((), ())),
                            preferred_element_type=f32) + bias_ref[g]
        s = mask_rows(s, 0, WINDOW - j * QUERY_BLOCK, first)
        s = mask_rows(s, BAND - (WINDOW - (n_qb - 1 - j) * QUERY_BLOCK), BAND, last)
        return s

    def softmax(g, s):
        sink = jnp.zeros((1, GROUP_LANES), f32)
        for r in range(Q_PER_KV):
            sink = jnp.where(lane_head == r, sink_ref[mixer, g * Q_PER_KV + r] * LOG2E, sink)
        m = jnp.maximum(jnp.max(s, axis=0, keepdims=True), sink)
        return jnp.exp2(s - m).astype(bf16), jnp.exp2(sink - m)

    def weighted_values(j, g, p, sink_p):
        ot = jnp.dot(vt_scr[g, :, j * QUERY_BLOCK:j * QUERY_BLOCK + BAND], p, preferred_element_type=f32)
        denom = ot[HEAD_DIM:HEAD_DIM + 1, :] + sink_p
        out_t = ot[:HEAD_DIM, :] * (1.0 / denom)
        for c in range(Q_PER_KV // 2):
            pair = jnp.concatenate([out_t[:, (2 * c) * QUERY_BLOCK:(2 * c + 1) * QUERY_BLOCK],
                                    out_t[:, (2 * c + 1) * QUERY_BLOCK:(2 * c + 2) * QUERY_BLOCK]], axis=0)
            col = (g * Q_PER_KV + 2 * c) * HEAD_DIM
            o_ref[j * QUERY_BLOCK:(j + 1) * QUERY_BLOCK, col:col + 2 * HEAD_DIM] = pair.T.astype(bf16)

    steps = [(j, g) for j in range(n_qb) for g in range(N_KV_HEADS)]
    s_next = scores(*steps[0])
    for n, (j, g) in enumerate(steps):
        s = s_next
        if n + 1 < len(steps):
            s_next = scores(*steps[n + 1])
        weighted_values(j, g, *softmax(g, s))


def _attention(qkv, bias, sink, mixer, seq_len):
    assert QUERY_BLOCK == LANES
    t = qkv.shape[0]
    tq = MIXER_TILE
    tiles_per_seq = seq_len // tq
    halo_per_tile = tq // WINDOW
    n_halo_blocks = t // WINDOW
    kv_col_block = Q_COLS // KV_COLS
    return pl.pallas_call(
        functools.partial(_attn_kernel, tiles_per_seq, mixer),
        out_shape=jax.ShapeDtypeStruct((t, Q_COLS), bf16),
        grid=(t // tq,),
        in_specs=[
            pl.BlockSpec((tq, QKV_COLS), lambda i: (i, 0)),
            pl.BlockSpec((WINDOW, KV_COLS),
                         lambda i: (jnp.maximum(i * halo_per_tile - 1, 0), kv_col_block)),
            pl.BlockSpec((WINDOW, KV_COLS),
                         lambda i: (jnp.minimum((i + 1) * halo_per_tile, n_halo_blocks - 1), kv_col_block)),
            _resident((N_KV_HEADS, BAND, GROUP_LANES)),
            pl.BlockSpec(memory_space=pltpu.SMEM),
        ],
        out_specs=pl.BlockSpec((tq, Q_COLS), lambda i: (i, 0)),
        scratch_shapes=[
            pltpu.VMEM((tq + 2 * WINDOW, K_COLS), bf16),
            pltpu.VMEM((N_KV_HEADS, LANES, tq + 2 * WINDOW), bf16),
        ],
        compiler_params=_params(),
        name="band_attention",
    )(qkv, qkv, qkv, bias, sink)


def _conv_kernel(tiles_per_seq, x_ref, prev_ref, next_ref, g_ref, win_ref, cw_ref, wout_ref, o_ref):
    n_sub = x_ref.shape[0] // SUB_ROWS
    i = pl.program_id(0)
    first = (i % tiles_per_seq) == 0
    last = (i % tiles_per_seq) == tiles_per_seq - 1
    g = g_ref[...]
    cw = cw_ref[...]
    edge = F32_SUBLANES
    z_main, b_gate = {}, {}
    z_edge = {}

    def rows(s):
        return slice(s * SUB_ROWS, (s + 1) * SUB_ROWS)

    def gate(s):
        h = _rms_norm(x_ref[rows(s), :], g).astype(bf16)
        parts, lo = [h], 0
        if s == 0:
            parts, lo = [_rms_norm(prev_ref[...], g).astype(bf16), h], CONV_HALO
        if s == n_sub - 1:
            parts = parts + [_rms_norm(next_ref[...], g).astype(bf16)]
        h_ext = parts[0] if len(parts) == 1 else jnp.concatenate(parts, axis=0)
        cu = jnp.dot(h_ext, win_ref[:, D_MODEL:], preferred_element_type=f32)
        z = cu[:, :D_MODEL] * cu[:, D_MODEL:]
        z_main[s] = z[lo:lo + SUB_ROWS]
        if s == 0:
            z_edge["lo"] = jnp.where(first, 0.0, z[lo - edge:lo])
        if s == n_sub - 1:
            z_edge["hi"] = jnp.where(last, 0.0, z[lo + SUB_ROWS:lo + SUB_ROWS + edge])
        b_gate[s] = jnp.dot(h, win_ref[:, :D_MODEL], preferred_element_type=f32)

    def output(s):
        z = z_main[s]
        below = z_edge["lo"] if s == 0 else z_main[s - 1][SUB_ROWS - edge:]
        above = z_edge["hi"] if s == n_sub - 1 else z_main[s + 1][:edge]
        z_ext = jnp.concatenate([below, z, above], axis=0)
        n_ext = SUB_ROWS + 2 * edge
        z_prev = pltpu.roll(z_ext, 1, axis=0)[edge:edge + SUB_ROWS]
        z_next = pltpu.roll(z_ext, n_ext - 1, axis=0)[edge:edge + SUB_ROWS]
        conv = z_prev * cw[0:1] + z * cw[1:2] + z_next * cw[2:3]
        y = jnp.dot((b_gate.pop(s) * conv).astype(bf16), wout_ref[...], preferred_element_type=f32)
        o_ref[rows(s), :] = x_ref[rows(s), :] + y
        z_main.pop(s - 1, None)

    gate(0)
    for s in range(1, n_sub):
        gate(s)
        output(s - 1)
    output(n_sub - 1)


def _short_conv(x, g, w_in, conv_w, w_out, layer, mixer, seq_len):
    t = x.shape[0]
    tm = MIXER_TILE
    tiles_per_seq = seq_len // tm
    halo_per_tile = tm // CONV_HALO
    n_halo_blocks = t // CONV_HALO
    return pl.pallas_call(
        functools.partial(_conv_kernel, tiles_per_seq),
        out_shape=jax.ShapeDtypeStruct((t, D_MODEL), f32),
        grid=(t // tm,),
        in_specs=[
            pl.BlockSpec((tm, D_MODEL), lambda i: (i, 0)),
            pl.BlockSpec((CONV_HALO, D_MODEL), lambda i: (jnp.maximum(i * halo_per_tile - 1, 0), 0)),
            pl.BlockSpec((CONV_HALO, D_MODEL),
                         lambda i: (jnp.minimum((i + 1) * halo_per_tile, n_halo_blocks - 1), 0)),
            _resident((1, D_MODEL), layer),
            _resident((D_MODEL, 3 * D_MODEL), mixer),
            _resident((CONV_WIDTH, D_MODEL), mixer),
            _resident((D_MODEL, D_MODEL), mixer),
        ],
        out_specs=pl.BlockSpec((tm, D_MODEL), lambda i: (i, 0)),
        compiler_params=_params(),
        name="short_conv",
    )(x, x, x, g, w_in, conv_w, w_out)


def _mlp_body(x_ref, att_ref, wo_ref, p_ref, gm_ref, gp_ref, gf_ref, up_ref, down_ref, wg_ref, wp_ref,
              o_ref, apply_final_norm):
    n_sub = x_ref.shape[0] // SUB_ROWS
    n_chunk = D_FF // FF_CHUNK
    state = {}

    def rows(s):
        return slice(s * SUB_ROWS, (s + 1) * SUB_ROWS)

    def open_(s):
        x = x_ref[rows(s), :]
        if att_ref is not None:
            x = x + jnp.dot(att_ref[rows(s), :], wo_ref[...], preferred_element_type=f32)
        state[s] = {"h": _rms_norm(x, gm_ref[...]).astype(bf16), "acc": x}

    def chunk(s, c):
        st = state[s]
        a = jnp.dot(st["h"], up_ref[:, c * FF_CHUNK:(c + 1) * FF_CHUNK], preferred_element_type=f32)
        a = jnp.square(jnp.maximum(a, 0.0)).astype(bf16)
        st["acc"] = st["acc"] + jnp.dot(a, down_ref[c * FF_CHUNK:(c + 1) * FF_CHUNK, :],
                                        preferred_element_type=f32)

    def close(s):
        x1 = state.pop(s)["acc"]
        z = jnp.dot(_rms_norm(x1, gp_ref[...]).astype(bf16), wg_ref[...], preferred_element_type=f32)
        gate = 1.0 / (1.0 + jnp.exp(-z))
        pp = jnp.dot(p_ref[rows(s), :].astype(bf16), wp_ref[...], preferred_element_type=f32)
        y = x1 + pp * gate
        if apply_final_norm:
            y = _rms_norm(y, gf_ref[...])
        o_ref[rows(s), :] = y

    open_(0)
    for s in range(n_sub):
        for c in range(n_chunk):
            if c == n_chunk - 1 and s + 1 < n_sub:
                open_(s + 1)
            chunk(s, c)
            if c == 0 and s > 0:
                close(s - 1)
    close(n_sub - 1)


def _mlp_kernel(apply_final_norm, x_ref, *refs):
    _mlp_body(x_ref, None, None, *refs, apply_final_norm=apply_final_norm)


def _attn_mlp_kernel(apply_final_norm, x_ref, att_ref, wo_ref, *refs):
    _mlp_body(x_ref, att_ref, wo_ref, *refs, apply_final_norm=apply_final_norm)


def _mlp_ple(x, att, w_o, mixer, p, layer, g_mlp, g_ple, g_final, w_up, w_down, w_gate, w_proj,
             apply_final_norm):
    t = x.shape[0]
    tm = TOKEN_TILE
    row_tile = pl.BlockSpec((tm, D_MODEL), lambda i: (i, 0))
    if att is None:
        body, lead_args, lead_specs = _mlp_kernel, (x,), [row_tile]
    else:
        body = _attn_mlp_kernel
        lead_args = (x, att, w_o)
        lead_specs = [row_tile, pl.BlockSpec((tm, Q_COLS), lambda i: (i, 0)),
                      _resident((Q_COLS, D_MODEL), mixer)]
    return pl.pallas_call(
        functools.partial(body, apply_final_norm),
        out_shape=jax.ShapeDtypeStruct((t, D_MODEL), f32),
        grid=(t // tm,),
        in_specs=lead_specs + [
            pl.BlockSpec((None, tm, PLE_DIM), lambda i: (layer, i, 0)),
            _resident((1, D_MODEL), layer),
            _resident((1, D_MODEL), layer),
            _resident((1, D_MODEL)),
            _resident((D_MODEL, D_FF), layer),
            _resident((D_FF, D_MODEL), layer),
            _resident((D_MODEL, D_MODEL), layer),
            _resident((PLE_DIM, D_MODEL), layer),
        ],
        out_specs=row_tile,
        compiler_params=_params(),
        name="mlp_ple",
    )(*lead_args, p, g_mlp, g_ple, g_final, w_up, w_down, w_gate, w_proj)


def _trunk(x, p, band_bias, w):
    b, s, _ = x.shape
    t = b * s
    x = x.reshape(t, D_MODEL)
    p = p.reshape(DEPTH, t, PLE_DIM)
    for i in range(DEPTH):
        j = i // N_MIXERS
        att = None
        if i % N_MIXERS == 0:
            qkv = _qkv_proj(x, w["norm_mix"], w["attn_w_qkv"], i, j)
            att = _attention(qkv, band_bias, w["attn_sink"], j, s)
        else:
            x = _short_conv(x, w["norm_mix"], w["conv_w_in"], w["conv_w"], w["conv_w_out"], i, j, s)
        x = _mlp_ple(x, att, w["attn_w_o"], j, p, i, w["norm_mlp"], w["norm_ple"], w["final_norm"],
                     w["mlp_w_up"], w["mlp_w_down"], w["ple_w_gate"], w["ple_w_proj"],
                     apply_final_norm=(i == DEPTH - 1))
    return x.reshape(b, s, D_MODEL)


def kernel(x_prompt, x_sample, p_prompt, p_sample, rel_bias, attn_w_qkv, attn_w_o, attn_sink,
           conv_w_in, conv_w, conv_w_out, mlp_w_up, mlp_w_down, ple_w_gate, ple_w_proj,
           norm_mix, norm_mlp, norm_ple, final_norm):
    w = {
        "attn_w_qkv": attn_w_qkv.astype(bf16),
        "attn_w_o": attn_w_o.astype(bf16),
        "attn_sink": attn_sink.astype(f32),
        "conv_w_in": conv_w_in.astype(bf16),
        "conv_w": conv_w.astype(f32),
        "conv_w_out": conv_w_out.astype(bf16),
        "mlp_w_up": mlp_w_up.astype(bf16),
        "mlp_w_down": mlp_w_down.astype(bf16),
        "ple_w_gate": ple_w_gate.astype(bf16),
        "ple_w_proj": ple_w_proj.astype(bf16),
        "norm_mix": norm_mix.astype(f32)[:, None, :],
        "norm_mlp": norm_mlp.astype(f32)[:, None, :],
        "norm_ple": norm_ple.astype(f32)[:, None, :],
        "final_norm": final_norm.astype(f32)[None, :],
    }
    band_bias = _band_bias(rel_bias)
    y_prompt = _trunk(x_prompt, p_prompt, band_bias, w)
    y_sample = _trunk(x_sample, p_sample, band_bias, w)
    return (y_prompt, y_sample)
```

```python
import functools
import math

import jax
import jax.numpy as jnp
from jax import lax
from jax.experimental import pallas as pl
from jax.experimental.pallas import tpu as pltpu

D_MODEL = 1024
DEPTH = 4
N_MIXERS = 2
HEAD_DIM = 64
N_HEADS = D_MODEL // HEAD_DIM
N_KV_HEADS = N_HEADS // 4
Q_PER_KV = N_HEADS // N_KV_HEADS
WINDOW = 128
N_BUCKETS = 32
MAX_DISTANCE = 128
CONV_WIDTH = 3
D_FF = 4 * D_MODEL
PLE_DIM = 256
EPS = 1e-6
NEG_INF = -1e30
LOG2E = math.log2(math.e)

Q_COLS = N_HEADS * HEAD_DIM
K_COLS = N_KV_HEADS * HEAD_DIM
KV_COLS = 2 * K_COLS
QKV_COLS = Q_COLS + KV_COLS
QUERY_BLOCK = 128
GROUP_LANES = Q_PER_KV * QUERY_BLOCK
BAND = QUERY_BLOCK + 2 * WINDOW

VMEM_LIMIT_BYTES = 56 * 1024 * 1024
BF16_SUBLANES = 16
F32_SUBLANES = 8
LANES = 128

TOKEN_TILE = 1024
MIXER_TILE = 2048
SUB_ROWS = 256
FF_CHUNK = 1024
CONV_HALO = BF16_SUBLANES

bf16 = jnp.bfloat16
f32 = jnp.float32


def _rms_norm(x, g):
    return x * lax.rsqrt(jnp.mean(x * x, axis=-1, keepdims=True) + EPS) * g


def _resident(shape, layer=None):
    if layer is None:
        return pl.BlockSpec(shape, lambda *_: (0,) * len(shape), pipeline_mode=pl.Buffered(1))
    return pl.BlockSpec((None,) + shape, lambda *_: (layer,) + (0,) * len(shape),
                        pipeline_mode=pl.Buffered(1))


def _params(n_axes=1):
    return pltpu.CompilerParams(
        dimension_semantics=("arbitrary",) * n_axes,
        vmem_limit_bytes=VMEM_LIMIT_BYTES,
    )


def _t5_bucket(rel):
    half = N_BUCKETS // 2
    max_exact = half // 2
    ret = jnp.where(rel > 0, half, 0)
    n = jnp.abs(rel)
    nf = jnp.maximum(n, 1).astype(jnp.float32)
    large = max_exact + (jnp.log(nf / max_exact) / math.log(MAX_DISTANCE / max_exact)
                         * (half - max_exact)).astype(jnp.int32)
    large = jnp.minimum(large, half - 1)
    return ret + jnp.where(n < max_exact, n, large)


def _bias_kernel(bucket_ref, rel_bias_ref, out_ref):
    bucket = bucket_ref[...]
    for g in range(N_KV_HEADS):
        for r in range(Q_PER_KV):
            h = g * Q_PER_KV + r
            acc = jnp.full(bucket.shape, NEG_INF, f32)
            for b in range(N_BUCKETS):
                acc = jnp.where(bucket == b, rel_bias_ref[b, h] * LOG2E, acc)
            out_ref[g, :, r * QUERY_BLOCK:(r + 1) * QUERY_BLOCK] = acc


def _band_bias(rel_bias):
    rel = (jnp.arange(BAND)[:, None] - WINDOW) - jnp.arange(QUERY_BLOCK)[None, :]
    bucket = jnp.where(jnp.abs(rel) <= WINDOW, _t5_bucket(rel), -1).astype(jnp.int32)
    return pl.pallas_call(
        _bias_kernel,
        out_shape=jax.ShapeDtypeStruct((N_KV_HEADS, BAND, GROUP_LANES), f32),
        in_specs=[pl.BlockSpec(memory_space=pltpu.VMEM), pl.BlockSpec(memory_space=pltpu.SMEM)],
        out_specs=pl.BlockSpec(memory_space=pltpu.VMEM),
        name="band_bias",
    )(bucket, rel_bias.astype(f32))


def _qkv_kernel(x_ref, g_ref, w_ref, o_ref):
    for s in range(x_ref.shape[0] // SUB_ROWS):
        rows = slice(s * SUB_ROWS, (s + 1) * SUB_ROWS)
        h = _rms_norm(x_ref[rows, :], g_ref[...]).astype(bf16)
        qkv = jnp.dot(h, w_ref[...], preferred_element_type=f32)
        o_ref[rows, :Q_COLS] = (qkv[:, :Q_COLS] * (HEAD_DIM ** -0.5 * LOG2E)).astype(bf16)
        o_ref[rows, Q_COLS:] = qkv[:, Q_COLS:].astype(bf16)


def _qkv_proj(x, g, w, layer, mixer):
    t = x.shape[0]
    return pl.pallas_call(
        _qkv_kernel,
        out_shape=jax.ShapeDtypeStruct((t, QKV_COLS), bf16),
        grid=(t // MIXER_TILE,),
        in_specs=[
            pl.BlockSpec((MIXER_TILE, D_MODEL), lambda i: (i, 0)),
            _resident((1, D_MODEL), layer),
            _resident((D_MODEL, QKV_COLS), mixer),
        ],
        out_specs=pl.BlockSpec((MIXER_TILE, QKV_COLS), lambda i: (i, 0)),
        compiler_params=_params(),
        name="qkv_proj",
    )(x, g, w)


def _attn_kernel(tiles_per_seq, mixer, qkv_ref, prev_ref, next_ref, bias_ref, sink_ref, o_ref, k_scr, vt_scr):
    tq = qkv_ref.shape[0]
    n_qb = tq // QUERY_BLOCK
    n_ext = tq + 2 * WINDOW
    i = pl.program_id(0)
    first = (i % tiles_per_seq) == 0
    last = (i % tiles_per_seq) == tiles_per_seq - 1

    k_scr[0:WINDOW, :] = prev_ref[:, :K_COLS]
    k_scr[WINDOW:WINDOW + tq, :] = qkv_ref[:, Q_COLS:Q_COLS + K_COLS]
    k_scr[WINDOW + tq:, :] = next_ref[:, :K_COLS]
    v_all = jnp.concatenate(
        [prev_ref[:, K_COLS:], qkv_ref[:, Q_COLS + K_COLS:], next_ref[:, K_COLS:]], axis=0)
    v_t = v_all.T
    ones = jnp.ones((LANES - HEAD_DIM, n_ext), bf16)
    for g in range(N_KV_HEADS):
        vt_scr[g, 0:HEAD_DIM, :] = v_t[g * HEAD_DIM:(g + 1) * HEAD_DIM, :]
        vt_scr[g, HEAD_DIM:, :] = ones

    lane_head = lax.broadcasted_iota(jnp.int32, (1, GROUP_LANES), 1) // QUERY_BLOCK

    def mask_rows(s, lo, hi, dead):
        if lo >= hi:
            return s
        parts = [s[:lo]] if lo else []
        parts.append(jnp.where(dead, NEG_INF, s[lo:hi]))
        if hi < BAND:
            parts.append(s[hi:])
        return jnp.concatenate(parts, axis=0)

    def scores(j, g):
        rows = slice(j * QUERY_BLOCK, (j + 1) * QUERY_BLOCK)
        q = jnp.concatenate(
            [qkv_ref[rows, (g * Q_PER_KV + r) * HEAD_DIM:(g * Q_PER_KV + r + 1) * HEAD_DIM]
             for r in range(Q_PER_KV)], axis=0)
        k = k_scr[j * QUERY_BLOCK:j * QUERY_BLOCK + BAND, g * HEAD_DIM:(g + 1) * HEAD_DIM]
        s = lax.dot_general(k, q, (((1,), (1,)), ((), ())),
                            preferred_element_type=f32) + bias_ref[g]
        s = mask_rows(s, 0, WINDOW - j * QUERY_BLOCK, first)
        s = mask_rows(s, BAND - (WINDOW - (n_qb - 1 - j) * QUERY_BLOCK), BAND, last)
        return s

    def softmax(g, s):
        sink = jnp.zeros((1, GROUP_LANES), f32)
        for r in range(Q_PER_KV):
            sink = jnp.where(lane_head == r, sink_ref[mixer, g * Q_PER_KV + r] * LOG2E, sink)
        m = jnp.maximum(jnp.max(s, axis=0, keepdims=True), sink)
        return jnp.exp2(s - m).astype(bf16), jnp.exp2(sink - m)

    def weighted_values(j, g, p, sink_p):
        ot = jnp.dot(vt_scr[g, :, j * QUERY_BLOCK:j * QUERY_BLOCK + BAND], p, preferred_element_type=f32)
        denom = ot[HEAD_DIM:HEAD_DIM + 1, :] + sink_p
        out_t = ot[:HEAD_DIM, :] * (1.0 / denom)
        for c in range(Q_PER_KV // 2):
            pair = jnp.concatenate([out_t[:, (2 * c) * QUERY_BLOCK:(2 * c + 1) * QUERY_BLOCK],
                                    out_t[:, (2 * c + 1) * QUERY_BLOCK:(2 * c + 2) * QUERY_BLOCK]], axis=0)
            col = (g * Q_PER_KV + 2 * c) * HEAD_DIM
            o_ref[j * QUERY_BLOCK:(j + 1) * QUERY_BLOCK, col:col + 2 * HEAD_DIM] = pair.T.astype(bf16)

    steps = [(j, g) for j in range(n_qb) for g in range(N_KV_HEADS)]
    s_next = scores(*steps[0])
    for n, (j, g) in enumerate(steps):
        s = s_next
        if n + 1 < len(steps):
            s_next = scores(*steps[n + 1])
        weighted_values(j, g, *softmax(g, s))


def _attention(qkv, bias, sink, mixer, seq_len):
    assert QUERY_BLOCK == LANES
    t = qkv.shape[0]
    tq = MIXER_TILE
    tiles_per_seq = seq_len // tq
    halo_per_tile = tq // WINDOW
    n_halo_blocks = t // WINDOW
    kv_col_block = Q_COLS // KV_COLS
    return pl.pallas_call(
        functools.partial(_attn_kernel, tiles_per_seq, mixer),
        out_shape=jax.ShapeDtypeStruct((t, Q_COLS), bf16),
        grid=(t // tq,),
        in_specs=[
            pl.BlockSpec((tq, QKV_COLS), lambda i: (i, 0)),
            pl.BlockSpec((WINDOW, KV_COLS),
                         lambda i: (jnp.maximum(i * halo_per_tile - 1, 0), kv_col_block)),
            pl.BlockSpec((WINDOW, KV_COLS),
                         lambda i: (jnp.minimum((i + 1) * halo_per_tile, n_halo_blocks - 1), kv_col_block)),
            _resident((N_KV_HEADS, BAND, GROUP_LANES)),
            pl.BlockSpec(memory_space=pltpu.SMEM),
        ],
        out_specs=pl.BlockSpec((tq, Q_COLS), lambda i: (i, 0)),
        scratch_shapes=[
            pltpu.VMEM((tq + 2 * WINDOW, K_COLS), bf16),
            pltpu.VMEM((N_KV_HEADS, LANES, tq + 2 * WINDOW), bf16),
        ],
        compiler_params=_params(),
        name="band_attention",
    )(qkv, qkv, qkv, bias, sink)


def _conv_kernel(tiles_per_seq, x_ref, prev_ref, next_ref, g_ref, win_ref, cw_ref, wout_ref, o_ref):
    n_sub = x_ref.shape[0] // SUB_ROWS
    i = pl.program_id(0)
    first = (i % tiles_per_seq) == 0
    last = (i % tiles_per_seq) == tiles_per_seq - 1
    g = g_ref[...]
    cw = cw_ref[...]
    edge = F32_SUBLANES
    z_main, b_gate = {}, {}
    z_edge = {}

    def rows(s):
        return slice(s * SUB_ROWS, (s + 1) * SUB_ROWS)

    def gate(s):
        h = _rms_norm(x_ref[rows(s), :], g).astype(bf16)
        parts, lo = [h], 0
        if s == 0:
            parts, lo = [_rms_norm(prev_ref[...], g).astype(bf16), h], CONV_HALO
        if s == n_sub - 1:
            parts = parts + [_rms_norm(next_ref[...], g).astype(bf16)]
        h_ext = parts[0] if len(parts) == 1 else jnp.concatenate(parts, axis=0)
        cu = jnp.dot(h_ext, win_ref[:, D_MODEL:], preferred_element_type=f32)
        z = cu[:, :D_MODEL] * cu[:, D_MODEL:]
        z_main[s] = z[lo:lo + SUB_ROWS]
        if s == 0:
            z_edge["lo"] = jnp.where(first, 0.0, z[lo - edge:lo])
        if s == n_sub - 1:
            z_edge["hi"] = jnp.where(last, 0.0, z[lo + SUB_ROWS:lo + SUB_ROWS + edge])
        b_gate[s] = jnp.dot(h, win_ref[:, :D_MODEL], preferred_element_type=f32)

    def output(s):
        z = z_main[s]
        below = z_edge["lo"] if s == 0 else z_main[s - 1][SUB_ROWS - edge:]
        above = z_edge["hi"] if s == n_sub - 1 else z_main[s + 1][:edge]
        z_ext = jnp.concatenate([below, z, above], axis=0)
        n_ext = SUB_ROWS + 2 * edge
        z_prev = pltpu.roll(z_ext, 1, axis=0)[edge:edge + SUB_ROWS]
        z_next = pltpu.roll(z_ext, n_ext - 1, axis=0)[edge:edge + SUB_ROWS]
        conv = z_prev * cw[0:1] + z * cw[1:2] + z_next * cw[2:3]
        y = jnp.dot((b_gate.pop(s) * conv).astype(bf16), wout_ref[...], preferred_element_type=f32)
        o_ref[rows(s), :] = x_ref[rows(s), :] + y
        z_main.pop(s - 1, None)

    gate(0)
    for s in range(1, n_sub):
        gate(s)
        output(s - 1)
    output(n_sub - 1)


def _short_conv(x, g, w_in, conv_w, w_out, layer, mixer, seq_len):
    t = x.shape[0]
    tm = MIXER_TILE
    tiles_per_seq = seq_len // tm
    halo_per_tile = tm // CONV_HALO
    n_halo_blocks = t // CONV_HALO
    return pl.pallas_call(
        functools.partial(_conv_kernel, tiles_per_seq),
        out_shape=jax.ShapeDtypeStruct((t, D_MODEL), f32),
        grid=(t // tm,),
        in_specs=[
            pl.BlockSpec((tm, D_MODEL), lambda i: (i, 0)),
            pl.BlockSpec((CONV_HALO, D_MODEL), lambda i: (jnp.maximum(i * halo_per_tile - 1, 0), 0)),
            pl.BlockSpec((CONV_HALO, D_MODEL),
                         lambda i: (jnp.minimum((i + 1) * halo_per_tile, n_halo_blocks - 1), 0)),
            _resident((1, D_MODEL), layer),
            _resident((D_MODEL, 3 * D_MODEL), mixer),
            _resident((CONV_WIDTH, D_MODEL), mixer),
            _resident((D_MODEL, D_MODEL), mixer),
        ],
        out_specs=pl.BlockSpec((tm, D_MODEL), lambda i: (i, 0)),
        compiler_params=_params(),
        name="short_conv",
    )(x, x, x, g, w_in, conv_w, w_out)


def _mlp_body(x_ref, att_ref, wo_ref, p_ref, gm_ref, gp_ref, gf_ref, up_ref, down_ref, wg_ref, wp_ref,
              o_ref, apply_final_norm):
    n_sub = x_ref.shape[0] // SUB_ROWS
    n_chunk = D_FF // FF_CHUNK
    state = {}

    def rows(s):
        return slice(s * SUB_ROWS, (s + 1) * SUB_ROWS)

    def open_(s):
        x = x_ref[rows(s), :]
        if att_ref is not None:
            x = x + jnp.dot(att_ref[rows(s), :], wo_ref[...], preferred_element_type=f32)
        state[s] = {"h": _rms_norm(x, gm_ref[...]).astype(bf16), "acc": x}

    def chunk(s, c):
        st = state[s]
        a = jnp.dot(st["h"], up_ref[:, c * FF_CHUNK:(c + 1) * FF_CHUNK], preferred_element_type=f32)
        a = jnp.square(jnp.maximum(a, 0.0)).astype(bf16)
        st["acc"] = st["acc"] + jnp.dot(a, down_ref[c * FF_CHUNK:(c + 1) * FF_CHUNK, :],
                                        preferred_element_type=f32)

    def close(s):
        x1 = state.pop(s)["acc"]
        pp = jnp.dot(p_ref[rows(s), :].astype(bf16), wp_ref[...], preferred_element_type=f32)
        z = jnp.dot(_rms_norm(x1, gp_ref[...]).astype(bf16), wg_ref[...], preferred_element_type=f32)
        gate = 1.0 / (1.0 + jnp.exp(-z))
        y = x1 + pp * gate
        if apply_final_norm:
            y = _rms_norm(y, gf_ref[...])
        o_ref[rows(s), :] = y

    open_(0)
    for s in range(n_sub):
        for c in range(n_chunk):
            if c == n_chunk - 1 and s + 1 < n_sub:
                open_(s + 1)
            chunk(s, c)
            if c == 0 and s > 0:
                close(s - 1)
    close(n_sub - 1)


def _mlp_kernel(apply_final_norm, x_ref, *refs):
    _mlp_body(x_ref, None, None, *refs, apply_final_norm=apply_final_norm)


def _attn_mlp_kernel(apply_final_norm, x_ref, att_ref, wo_ref, *refs):
    _mlp_body(x_ref, att_ref, wo_ref, *refs, apply_final_norm=apply_final_norm)


def _mlp_ple(x, att, w_o, mixer, p, layer, g_mlp, g_ple, g_final, w_up, w_down, w_gate, w_proj,
             apply_final_norm):
    t = x.shape[0]
    tm = TOKEN_TILE
    row_tile = pl.BlockSpec((tm, D_MODEL), lambda i: (i, 0))
    if att is None:
        body, lead_args, lead_specs = _mlp_kernel, (x,), [row_tile]
    else:
        body = _attn_mlp_kernel
        lead_args = (x, att, w_o)
        lead_specs = [row_tile, pl.BlockSpec((tm, Q_COLS), lambda i: (i, 0)),
                      _resident((Q_COLS, D_MODEL), mixer)]
    return pl.pallas_call(
        functools.partial(body, apply_final_norm),
        out_shape=jax.ShapeDtypeStruct((t, D_MODEL), f32),
        grid=(t // tm,),
        in_specs=lead_specs + [
            pl.BlockSpec((None, tm, PLE_DIM), lambda i: (layer, i, 0)),
            _resident((1, D_MODEL), layer),
            _resident((1, D_MODEL), layer),
            _resident((1, D_MODEL)),
            _resident((D_MODEL, D_FF), layer),
            _resident((D_FF, D_MODEL), layer),
            _resident((D_MODEL, D_MODEL), layer),
            _resident((PLE_DIM, D_MODEL), layer),
        ],
        out_specs=row_tile,
        compiler_params=_params(),
        name="mlp_ple",
    )(*lead_args, p, g_mlp, g_ple, g_final, w_up, w_down, w_gate, w_proj)


def _trunk(x, p, band_bias, w):
    b, s, _ = x.shape
    t = b * s
    x = x.reshape(t, D_MODEL)
    p = p.reshape(DEPTH, t, PLE_DIM)
    for i in range(DEPTH):
        j = i // N_MIXERS
        att = None
        if i % N_MIXERS == 0:
            qkv = _qkv_proj(x, w["norm_mix"], w["attn_w_qkv"], i, j)
            att = _attention(qkv, band_bias, w["attn_sink"], j, s)
        else:
            x = _short_conv(x, w["norm_mix"], w["conv_w_in"], w["conv_w"], w["conv_w_out"], i, j, s)
        x = _mlp_ple(x, att, w["attn_w_o"], j, p, i, w["norm_mlp"], w["norm_ple"], w["final_norm"],
                     w["mlp_w_up"], w["mlp_w_down"], w["ple_w_gate"], w["ple_w_proj"],
                     apply_final_norm=(i == DEPTH - 1))
    return x.reshape(b, s, D_MODEL)


def kernel(x_prompt, x_sample, p_prompt, p_sample, rel_bias, attn_w_qkv, attn_w_o, attn_sink,
           conv_w_in, conv_w, conv_w_out, mlp_w_up, mlp_w_down, ple_w_gate, ple_w_proj,
           norm_mix, norm_mlp, norm_ple, final_norm):
    w = {
        "attn_w_qkv": attn_w_qkv.astype(bf16),
        "attn_w_o": attn_w_o.astype(bf16),
        "attn_sink": attn_sink.astype(f32),
        "conv_w_in": conv_w_in.astype(bf16),
        "conv_w": conv_w.astype(f32),
        "conv_w_out": conv_w_out.astype(bf16),
        "mlp_w_up": mlp_w_up.astype(bf16),
        "mlp_w_down": mlp_w_down.astype(bf16),
        "ple_w_gate": ple_w_gate.astype(bf16),
        "ple_w_proj": ple_w_proj.astype(bf16),
        "norm_mix": norm_mix.astype(f32)[:, None, :],
        "norm_mlp": norm_mlp.astype(f32)[:, None, :],
        "norm_ple": norm_ple.astype(f32)[:, None, :],
        "final_norm": final_norm.astype(f32)[None, :],
    }
    band_bias = _band_bias(rel_bias)
    y_prompt = _trunk(x_prompt, p_prompt, band_bias, w)
    y_sample = _trunk(x_sample, p_sample, band_bias, w)
    return (y_prompt, y_sample)
```

```python
import functools
import math

import jax
import jax.numpy as jnp
from jax import lax
from jax.experimental import pallas as pl
from jax.experimental.pallas import tpu as pltpu

D_MODEL = 1024
DEPTH = 4
N_MIXERS = 2
HEAD_DIM = 64
N_HEADS = D_MODEL // HEAD_DIM
N_KV_HEADS = N_HEADS // 4
Q_PER_KV = N_HEADS // N_KV_HEADS
WINDOW = 128
N_BUCKETS = 32
MAX_DISTANCE = 128
CONV_WIDTH = 3
D_FF = 4 * D_MODEL
PLE_DIM = 256
EPS = 1e-6
NEG_INF = -1e30
LOG2E = math.log2(math.e)

Q_COLS = N_HEADS * HEAD_DIM
K_COLS = N_KV_HEADS * HEAD_DIM
KV_COLS = 2 * K_COLS
QKV_COLS = Q_COLS + KV_COLS
QUERY_BLOCK = 128
GROUP_LANES = Q_PER_KV * QUERY_BLOCK
BAND = QUERY_BLOCK + 2 * WINDOW

VMEM_LIMIT_BYTES = 56 * 1024 * 1024
BF16_SUBLANES = 16
F32_SUBLANES = 8
LANES = 128

TOKEN_TILE = 1024
MIXER_TILE = 2048
SUB_ROWS = 256
FF_CHUNK = 1024
CONV_HALO = BF16_SUBLANES

bf16 = jnp.bfloat16
f32 = jnp.float32


def _rms_norm(x, g):
    return x * lax.rsqrt(jnp.mean(x * x, axis=-1, keepdims=True) + EPS) * g


def _resident(shape, layer=None):
    if layer is None:
        return pl.BlockSpec(shape, lambda *_: (0,) * len(shape), pipeline_mode=pl.Buffered(1))
    return pl.BlockSpec((None,) + shape, lambda *_: (layer,) + (0,) * len(shape),
                        pipeline_mode=pl.Buffered(1))


def _params(n_axes=1):
    return pltpu.CompilerParams(
        dimension_semantics=("arbitrary",) * n_axes,
        vmem_limit_bytes=VMEM_LIMIT_BYTES,
    )


def _t5_bucket(rel):
    half = N_BUCKETS // 2
    max_exact = half // 2
    ret = jnp.where(rel > 0, half, 0)
    n = jnp.abs(rel)
    nf = jnp.maximum(n, 1).astype(jnp.float32)
    large = max_exact + (jnp.log(nf / max_exact) / math.log(MAX_DISTANCE / max_exact)
                         * (half - max_exact)).astype(jnp.int32)
    large = jnp.minimum(large, half - 1)
    return ret + jnp.where(n < max_exact, n, large)


def _bias_kernel(bucket_ref, rel_bias_ref, out_ref):
    bucket = bucket_ref[...]
    for g in range(N_KV_HEADS):
        for r in range(Q_PER_KV):
            h = g * Q_PER_KV + r
            acc = jnp.full(bucket.shape, NEG_INF, f32)
            for b in range(N_BUCKETS):
                acc = jnp.where(bucket == b, rel_bias_ref[b, h] * LOG2E, acc)
            out_ref[g, :, r * QUERY_BLOCK:(r + 1) * QUERY_BLOCK] = acc


def _band_bias(rel_bias):
    rel = (jnp.arange(BAND)[:, None] - WINDOW) - jnp.arange(QUERY_BLOCK)[None, :]
    bucket = jnp.where(jnp.abs(rel) <= WINDOW, _t5_bucket(rel), -1).astype(jnp.int32)
    return pl.pallas_call(
        _bias_kernel,
        out_shape=jax.ShapeDtypeStruct((N_KV_HEADS, BAND, GROUP_LANES), f32),
        in_specs=[pl.BlockSpec(memory_space=pltpu.VMEM), pl.BlockSpec(memory_space=pltpu.SMEM)],
        out_specs=pl.BlockSpec(memory_space=pltpu.VMEM),
        name="band_bias",
    )(bucket, rel_bias.astype(f32))


def _qkv_kernel(x_ref, g_ref, w_ref, o_ref):
    for s in range(x_ref.shape[0] // SUB_ROWS):
        rows = slice(s * SUB_ROWS, (s + 1) * SUB_ROWS)
        h = _rms_norm(x_ref[rows, :], g_ref[...]).astype(bf16)
        qkv = jnp.dot(h, w_ref[...], preferred_element_type=f32)
        o_ref[rows, :Q_COLS] = (qkv[:, :Q_COLS] * (HEAD_DIM ** -0.5 * LOG2E)).astype(bf16)
        o_ref[rows, Q_COLS:] = qkv[:, Q_COLS:].astype(bf16)


def _qkv_proj(x, g, w, layer, mixer):
    t = x.shape[0]
    return pl.pallas_call(
        _qkv_kernel,
        out_shape=jax.ShapeDtypeStruct((t, QKV_COLS), bf16),
        grid=(t // MIXER_TILE,),
        in_specs=[
            pl.BlockSpec((MIXER_TILE, D_MODEL), lambda i: (i, 0)),
            _resident((1, D_MODEL), layer),
            _resident((D_MODEL, QKV_COLS), mixer),
        ],
        out_specs=pl.BlockSpec((MIXER_TILE, QKV_COLS), lambda i: (i, 0)),
        compiler_params=_params(),
        name="qkv_proj",
    )(x, g, w)


def _attn_kernel(tiles_per_seq, mixer, qkv_ref, prev_ref, next_ref, bias_ref, sink_ref, o_ref, k_scr, vt_scr):
    tq = qkv_ref.shape[0]
    n_qb = tq // QUERY_BLOCK
    n_ext = tq + 2 * WINDOW
    i = pl.program_id(0)
    first = (i % tiles_per_seq) == 0
    last = (i % tiles_per_seq) == tiles_per_seq - 1

    k_scr[0:WINDOW, :] = prev_ref[:, :K_COLS]
    k_scr[WINDOW:WINDOW + tq, :] = qkv_ref[:, Q_COLS:Q_COLS + K_COLS]
    k_scr[WINDOW + tq:, :] = next_ref[:, :K_COLS]
    v_all = jnp.concatenate(
        [prev_ref[:, K_COLS:], qkv_ref[:, Q_COLS + K_COLS:], next_ref[:, K_COLS:]], axis=0)
    v_t = v_all.T
    ones = jnp.ones((LANES - HEAD_DIM, n_ext), bf16)
    for g in range(N_KV_HEADS):
        vt_scr[g, 0:HEAD_DIM, :] = v_t[g * HEAD_DIM:(g + 1) * HEAD_DIM, :]
        vt_scr[g, HEAD_DIM:, :] = ones

    lane_head = lax.broadcasted_iota(jnp.int32, (1, GROUP_LANES), 1) // QUERY_BLOCK

    def mask_rows(s, lo, hi, dead):
        if lo >= hi:
            return s
        parts = [s[:lo]] if lo else []
        parts.append(jnp.where(dead, NEG_INF, s[lo:hi]))
        if hi < BAND:
            parts.append(s[hi:])
        return jnp.concatenate(parts, axis=0)

    def scores(j, g):
        rows = slice(j * QUERY_BLOCK, (j + 1) * QUERY_BLOCK)
        q = jnp.concatenate(
            [qkv_ref[rows, (g * Q_PER_KV + r) * HEAD_DIM:(g * Q_PER_KV + r + 1) * HEAD_DIM]
             for r in range(Q_PER_KV)], axis=0)
        k = k_scr[j * QUERY_BLOCK:j * QUERY_BLOCK + BAND, g * HEAD_DIM:(g + 1) * HEAD_DIM]
        s = lax.dot_general(k, q, (((1,), (1,)), ((), ())),
                            preferred_element_type=f32) + bias_ref[g]
        s = mask_rows(s, 0, WINDOW - j * QUERY_BLOCK, first)
        s = mask_rows(s, BAND - (WINDOW - (n_qb - 1 - j) * QUERY_BLOCK), BAND, last)
        return s

    def softmax(g, s):
        sink = jnp.zeros((1, GROUP_LANES), f32)
        for r in range(Q_PER_KV):
            sink = jnp.where(lane_head == r, sink_ref[mixer, g * Q_PER_KV + r] * LOG2E, sink)
        m = jnp.maximum(jnp.max(s, axis=0, keepdims=True), sink)
        return jnp.exp2(s - m).astype(bf16), jnp.exp2(sink - m)

    def weighted_values(j, g, p, sink_p):
        ot = jnp.dot(vt_scr[g, :, j * QUERY_BLOCK:j * QUERY_BLOCK + BAND], p, preferred_element_type=f32)
        denom = ot[HEAD_DIM:HEAD_DIM + 1, :] + sink_p
        out_t = ot[:HEAD_DIM, :] * (1.0 / denom)
        for c in range(Q_PER_KV // 2):
            pair = jnp.concatenate([out_t[:, (2 * c) * QUERY_BLOCK:(2 * c + 1) * QUERY_BLOCK],
                                    out_t[:, (2 * c + 1) * QUERY_BLOCK:(2 * c + 2) * QUERY_BLOCK]], axis=0)
            col = (g * Q_PER_KV + 2 * c) * HEAD_DIM
            o_ref[j * QUERY_BLOCK:(j + 1) * QUERY_BLOCK, col:col + 2 * HEAD_DIM] = pair.T.astype(bf16)

    steps = [(j, g) for j in range(n_qb) for g in range(N_KV_HEADS)]
    s_next = scores(*steps[0])
    pending = None
    for n, (j, g) in enumerate(steps):
        s = s_next
        if n + 1 < len(steps):
            s_next = scores(*steps[n + 1])
        p, sink_p = softmax(g, s)
        if pending is not None:
            weighted_values(*pending)
        pending = (j, g, p, sink_p)
    weighted_values(*pending)


def _attention(qkv, bias, sink, mixer, seq_len):
    assert QUERY_BLOCK == LANES
    t = qkv.shape[0]
    tq = MIXER_TILE
    tiles_per_seq = seq_len // tq
    halo_per_tile = tq // WINDOW
    n_halo_blocks = t // WINDOW
    kv_col_block = Q_COLS // KV_COLS
    return pl.pallas_call(
        functools.partial(_attn_kernel, tiles_per_seq, mixer),
        out_shape=jax.ShapeDtypeStruct((t, Q_COLS), bf16),
        grid=(t // tq,),
        in_specs=[
            pl.BlockSpec((tq, QKV_COLS), lambda i: (i, 0)),
            pl.BlockSpec((WINDOW, KV_COLS),
                         lambda i: (jnp.maximum(i * halo_per_tile - 1, 0), kv_col_block)),
            pl.BlockSpec((WINDOW, KV_COLS),
                         lambda i: (jnp.minimum((i + 1) * halo_per_tile, n_halo_blocks - 1), kv_col_block)),
            _resident((N_KV_HEADS, BAND, GROUP_LANES)),
            pl.BlockSpec(memory_space=pltpu.SMEM),
        ],
        out_specs=pl.BlockSpec((tq, Q_COLS), lambda i: (i, 0)),
        scratch_shapes=[
            pltpu.VMEM((tq + 2 * WINDOW, K_COLS), bf16),
            pltpu.VMEM((N_KV_HEADS, LANES, tq + 2 * WINDOW), bf16),
        ],
        compiler_params=_params(),
        name="band_attention",
    )(qkv, qkv, qkv, bias, sink)


def _conv_kernel(tiles_per_seq, x_ref, prev_ref, next_ref, g_ref, win_ref, cw_ref, wout_ref, o_ref):
    n_sub = x_ref.shape[0] // SUB_ROWS
    i = pl.program_id(0)
    first = (i % tiles_per_seq) == 0
    last = (i % tiles_per_seq) == tiles_per_seq - 1
    g = g_ref[...]
    cw = cw_ref[...]
    edge = F32_SUBLANES
    z_main, b_gate = {}, {}
    z_edge = {}

    def rows(s):
        return slice(s * SUB_ROWS, (s + 1) * SUB_ROWS)

    def gate(s):
        h = _rms_norm(x_ref[rows(s), :], g).astype(bf16)
        parts, lo = [h], 0
        if s == 0:
            parts, lo = [_rms_norm(prev_ref[...], g).astype(bf16), h], CONV_HALO
        if s == n_sub - 1:
            parts = parts + [_rms_norm(next_ref[...], g).astype(bf16)]
        h_ext = parts[0] if len(parts) == 1 else jnp.concatenate(parts, axis=0)
        cu = jnp.dot(h_ext, win_ref[:, D_MODEL:], preferred_element_type=f32)
        z = cu[:, :D_MODEL] * cu[:, D_MODEL:]
        z_main[s] = z[lo:lo + SUB_ROWS]
        if s == 0:
            z_edge["lo"] = jnp.where(first, 0.0, z[lo - edge:lo])
        if s == n_sub - 1:
            z_edge["hi"] = jnp.where(last, 0.0, z[lo + SUB_ROWS:lo + SUB_ROWS + edge])
        b_gate[s] = jnp.dot(h, win_ref[:, :D_MODEL], preferred_element_type=f32)

    def output(s):
        z = z_main[s]
        below = z_edge["lo"] if s == 0 else z_main[s - 1][SUB_ROWS - edge:]
        above = z_edge["hi"] if s == n_sub - 1 else z_main[s + 1][:edge]
        z_ext = jnp.concatenate([below, z, above], axis=0)
        n_ext = SUB_ROWS + 2 * edge
        z_prev = pltpu.roll(z_ext, 1, axis=0)[edge:edge + SUB_ROWS]
        z_next = pltpu.roll(z_ext, n_ext - 1, axis=0)[edge:edge + SUB_ROWS]
        conv = z_prev * cw[0:1] + z * cw[1:2] + z_next * cw[2:3]
        y = jnp.dot((b_gate.pop(s) * conv).astype(bf16), wout_ref[...], preferred_element_type=f32)
        o_ref[rows(s), :] = x_ref[rows(s), :] + y
        z_main.pop(s - 1, None)

    gate(0)
    for s in range(1, n_sub):
        gate(s)
        output(s - 1)
    output(n_sub - 1)


def _short_conv(x, g, w_in, conv_w, w_out, layer, mixer, seq_len):
    t = x.shape[0]
    tm = MIXER_TILE
    tiles_per_seq = seq_len // tm
    halo_per_tile = tm // CONV_HALO
    n_halo_blocks = t // CONV_HALO
    return pl.pallas_call(
        functools.partial(_conv_kernel, tiles_per_seq),
        out_shape=jax.ShapeDtypeStruct((t, D_MODEL), f32),
        grid=(t // tm,),
        in_specs=[
            pl.BlockSpec((tm, D_MODEL), lambda i: (i, 0)),
            pl.BlockSpec((CONV_HALO, D_MODEL), lambda i: (jnp.maximum(i * halo_per_tile - 1, 0), 0)),
            pl.BlockSpec((CONV_HALO, D_MODEL),
                         lambda i: (jnp.minimum((i + 1) * halo_per_tile, n_halo_blocks - 1), 0)),
            _resident((1, D_MODEL), layer),
            _resident((D_MODEL, 3 * D_MODEL), mixer),
            _resident((CONV_WIDTH, D_MODEL), mixer),
            _resident((D_MODEL, D_MODEL), mixer),
        ],
        out_specs=pl.BlockSpec((tm, D_MODEL), lambda i: (i, 0)),
        compiler_params=_params(),
        name="short_conv",
    )(x, x, x, g, w_in, conv_w, w_out)


def _mlp_body(x_ref, att_ref, wo_ref, p_ref, gm_ref, gp_ref, gf_ref, up_ref, down_ref, wg_ref, wp_ref,
              o_ref, apply_final_norm):
    n_sub = x_ref.shape[0] // SUB_ROWS
    n_chunk = D_FF // FF_CHUNK
    state = {}

    def rows(s):
        return slice(s * SUB_ROWS, (s + 1) * SUB_ROWS)

    def open_(s):
        x = x_ref[rows(s), :]
        if att_ref is not None:
            x = x + jnp.dot(att_ref[rows(s), :], wo_ref[...], preferred_element_type=f32)
        state[s] = {"h": _rms_norm(x, gm_ref[...]).astype(bf16), "acc": x}

    def chunk(s, c):
        st = state[s]
        a = jnp.dot(st["h"], up_ref[:, c * FF_CHUNK:(c + 1) * FF_CHUNK], preferred_element_type=f32)
        a = jnp.square(jnp.maximum(a, 0.0)).astype(bf16)
        st["acc"] = st["acc"] + jnp.dot(a, down_ref[c * FF_CHUNK:(c + 1) * FF_CHUNK, :],
                                        preferred_element_type=f32)

    def close(s):
        x1 = state.pop(s)["acc"]
        pp = jnp.dot(p_ref[rows(s), :].astype(bf16), wp_ref[...], preferred_element_type=f32)
        z = jnp.dot(_rms_norm(x1, gp_ref[...]).astype(bf16), wg_ref[...], preferred_element_type=f32)
        gate = 1.0 / (1.0 + jnp.exp(-z))
        y = x1 + pp * gate
        if apply_final_norm:
            y = _rms_norm(y, gf_ref[...])
        o_ref[rows(s), :] = y

    open_(0)
    for s in range(n_sub):
        for c in range(n_chunk):
            if c == n_chunk - 1 and s + 1 < n_sub:
                open_(s + 1)
            chunk(s, c)
            if c == 0 and s > 0:
                close(s - 1)
    close(n_sub - 1)


def _mlp_kernel(apply_final_norm, x_ref, *refs):
    _mlp_body(x_ref, None, None, *refs, apply_final_norm=apply_final_norm)


def _attn_mlp_kernel(apply_final_norm, x_ref, att_ref, wo_ref, *refs):
    _mlp_body(x_ref, att_ref, wo_ref, *refs, apply_final_norm=apply_final_norm)


def _mlp_ple(x, att, w_o, mixer, p, layer, g_mlp, g_ple, g_final, w_up, w_down, w_gate, w_proj,
             apply_final_norm):
    t = x.shape[0]
    tm = TOKEN_TILE
    row_tile = pl.BlockSpec((tm, D_MODEL), lambda i: (i, 0))
    if att is None:
        body, lead_args, lead_specs = _mlp_kernel, (x,), [row_tile]
    else:
        body = _attn_mlp_kernel
        lead_args = (x, att, w_o)
        lead_specs = [row_tile, pl.BlockSpec((tm, Q_COLS), lambda i: (i, 0)),
                      _resident((Q_COLS, D_MODEL), mixer)]
    return pl.pallas_call(
        functools.partial(body, apply_final_norm),
        out_shape=jax.ShapeDtypeStruct((t, D_MODEL), f32),
        grid=(t // tm,),
        in_specs=lead_specs + [
            pl.BlockSpec((None, tm, PLE_DIM), lambda i: (layer, i, 0)),
            _resident((1, D_MODEL), layer),
            _resident((1, D_MODEL), layer),
            _resident((1, D_MODEL)),
            _resident((D_MODEL, D_FF), layer),
            _resident((D_FF, D_MODEL), layer),
            _resident((D_MODEL, D_MODEL), layer),
            _resident((PLE_DIM, D_MODEL), layer),
        ],
        out_specs=row_tile,
        compiler_params=_params(),
        name="mlp_ple",
    )(*lead_args, p, g_mlp, g_ple, g_final, w_up, w_down, w_gate, w_proj)


def _trunk(x, p, band_bias, w):
    b, s, _ = x.shape
    t = b * s
    x = x.reshape(t, D_MODEL)
    p = p.reshape(DEPTH, t, PLE_DIM)
    for i in range(DEPTH):
        j = i // N_MIXERS
        att = None
        if i % N_MIXERS == 0:
            qkv = _qkv_proj(x, w["norm_mix"], w["attn_w_qkv"], i, j)
            att = _attention(qkv, band_bias, w["attn_sink"], j, s)
        else:
            x = _short_conv(x, w["norm_mix"], w["conv_w_in"], w["conv_w"], w["conv_w_out"], i, j, s)
        x = _mlp_ple(x, att, w["attn_w_o"], j, p, i, w["norm_mlp"], w["norm_ple"], w["final_norm"],
                     w["mlp_w_up"], w["mlp_w_down"], w["ple_w_gate"], w["ple_w_proj"],
                     apply_final_norm=(i == DEPTH - 1))
    return x.reshape(b, s, D_MODEL)


def kernel(x_prompt, x_sample, p_prompt, p_sample, rel_bias, attn_w_qkv, attn_w_o, attn_sink,
           conv_w_in, conv_w, conv_w_out, mlp_w_up, mlp_w_down, ple_w_gate, ple_w_proj,
           norm_mix, norm_mlp, norm_ple, final_norm):
    w = {
        "attn_w_qkv": attn_w_qkv.astype(bf16),
        "attn_w_o": attn_w_o.astype(bf16),
        "attn_sink": attn_sink.astype(f32),
        "conv_w_in": conv_w_in.astype(bf16),
        "conv_w": conv_w.astype(f32),
        "conv_w_out": conv_w_out.astype(bf16),
        "mlp_w_up": mlp_w_up.astype(bf16),
        "mlp_w_down": mlp_w_down.astype(bf16),
        "ple_w_gate": ple_w_gate.astype(bf16),
        "ple_w_proj": ple_w_proj.astype(bf16),
        "norm_mix": norm_mix.astype(f32)[:, None, :],
        "norm_mlp": norm_mlp.astype(f32)[:, None, :],
        "norm_ple": norm_ple.astype(f32)[:, None, :],
        "final_norm": final_norm.astype(f32)[None, :],
    }
    band_bias = _band_bias(rel_bias)
    y_prompt = _trunk(x_prompt, p_prompt, band_bias, w)
    y_sample = _trunk(x_sample, p_sample, band_bias, w)
    return (y_prompt, y_sample)
```
